```python
import jax, jax.numpy as jnp
from jax import lax
import numpy as np

D_MODEL = 1024
BATCH = 4
SEQ = 8192
DEPTH = 4

GRID_W = 64
CTX_LEN = 256
N_MIXERS = 3
N_MOD = 6
EPS = 1e-6
F32 = jnp.float32

GLA_HEADS = 4
GLA_DK = D_MODEL // 2
GLA_DV = D_MODEL
GLA_HK = GLA_DK // GLA_HEADS
GLA_HV = GLA_DV // GLA_HEADS
GLA_GATE_RANK = 16
GLA_TAU = 16.0
GLA_CHUNK = 64
GLA_IN = 2 * GLA_DK + 2 * GLA_DV

MLA_HEADS = 8
MLA_NOPE = 128
MLA_ROPE = 64
MLA_QK = MLA_NOPE + MLA_ROPE
MLA_V = 128
MLA_Q_RANK = 256
MLA_KV_RANK = 128
MLA_IN = MLA_Q_RANK + MLA_KV_RANK + MLA_ROPE
MLA_QBLOCK = 128
ROPE_THETA = 10000.0

SSM_INNER = 2 * D_MODEL
SSM_HEADDIM = 64
SSM_HEADS = SSM_INNER // SSM_HEADDIM
SSM_STATE = 128
SSM_GROUPS = 4
SSM_CONV = 3
SSM_CHUNK = 64
SSM_BC = SSM_GROUPS * SSM_STATE
SSM_XBC = SSM_INNER + 2 * SSM_BC
SSM_IN = SSM_INNER + SSM_XBC + 2 * SSM_HEADS

FFN_HIDDEN = 2816
FFN_CONV = 3

N_GLA = len(range(0, DEPTH, N_MIXERS))
N_MLA = len(range(1, DEPTH, N_MIXERS))
N_SSM = len(range(2, DEPTH, N_MIXERS))

kernel_name = 'hybrid_gla_mla_ssd_prefix_dit'


def rmsnorm(t, w):
    tf = t.astype(F32)
    tf = tf * lax.rsqrt(jnp.mean(tf * tf, axis=-1, keepdims=True) + EPS)
    return (tf * w.astype(F32)).astype(t.dtype)


def modulation(cond, w, b):
    m = jax.nn.silu(cond) @ w + b
    return jnp.split(m[..., None, :], N_MOD, axis=-1)


def to_heads(t, n):
    b, l, _ = t.shape
    return t.reshape(b, l, n, -1).transpose(0, 2, 1, 3)


def from_heads(t):
    b, n, l, d = t.shape
    return t.transpose(0, 2, 1, 3).reshape(b, l, n * d)


def dwconv_centred(x, w, b):
    k, ch = w.shape
    y = lax.conv_general_dilated(x, w.reshape(k, 1, ch), window_strides=(1,),
                                 padding=[(k // 2, k // 2)],
                                 dimension_numbers=('NWC', 'WIO', 'NWC'),
                                 feature_group_count=ch)
    return y + b


def axial_rope(rows):
    pos_r = jnp.repeat(jnp.arange(rows, dtype=F32), GRID_W)
    pos_c = jnp.tile(jnp.arange(GRID_W, dtype=F32), rows)
    n_freq = MLA_ROPE // 4
    inv = ROPE_THETA ** (-jnp.arange(n_freq, dtype=F32) / n_freq)
    ang = jnp.concatenate([pos_r[:, None] * inv, pos_c[:, None] * inv], axis=-1)
    return jnp.cos(ang), jnp.sin(ang)


def rope_tail(t, cos, sin):
    nope, pe = jnp.split(t, [MLA_NOPE], axis=-1)
    p1, p2 = jnp.split(pe.astype(F32), 2, axis=-1)
    cs, sn = cos[:, None, :], sin[:, None, :]
    pe = jnp.concatenate([p1 * cs - p2 * sn, p2 * cs + p1 * sn], axis=-1).astype(t.dtype)
    return jnp.concatenate([nope, pe], axis=-1)


def block_attention(q, k, v):
    b, h, lq, dq = q.shape
    n = lq // MLA_QBLOCK
    qb = jnp.moveaxis(q.reshape(b, h, n, MLA_QBLOCK, dq), 2, 0)
    scale = dq ** -0.5
    def one(qi):
        s = jnp.einsum('bhqd,bhkd->bhqk', qi, k).astype(F32) * scale
        p = jax.nn.softmax(s, axis=-1)
        return jnp.einsum('bhqk,bhkd->bhqd', p.astype(v.dtype), v)
    o = lax.map(one, qb)
    return jnp.moveaxis(o, 0, 2).reshape(b, h, lq, v.shape[-1])


def gla_chunk_scan(q, k, v, lg, s0):
    b, h, l, _ = q.shape
    n = l // GLA_CHUNK
    mask = jnp.tril(jnp.ones((GLA_CHUNK, GLA_CHUNK), bool))[:, :, None]
    def chunks(t):
        return jnp.moveaxis(t.reshape(b, h, n, GLA_CHUNK, t.shape[-1]), 2, 0)
    def step(s, inp):
        qc, kc, vc, gc = (t.astype(F32) for t in inp)
        g = jnp.cumsum(gc, axis=2)
        diff = g[:, :, :, None, :] - g[:, :, None, :, :]
        decay = jnp.exp(jnp.where(mask, diff, -jnp.inf))
        att = jnp.einsum('bhik,bhjk,bhijk->bhij', qc, kc, decay)
        o = jnp.einsum('bhij,bhjv->bhiv', att, vc) + jnp.einsum('bhik,bhkv->bhiv', qc * jnp.exp(g), s)
        g_last = g[:, :, -1:, :]
        s = s * jnp.exp(g_last[:, :, 0, :, None]) + jnp.einsum('bhjk,bhjv->bhkv', kc * jnp.exp(g_last - g), vc)
        return s, o
    _, o = lax.scan(step, s0, (chunks(q), chunks(k), chunks(v), chunks(lg)))
    return jnp.moveaxis(o, 0, 2).reshape(b, h, l, v.shape[-1])


def gla_final_state(k, v, lg):
    g = jnp.cumsum(lg, axis=2)
    w = jnp.exp(g[:, :, -1:, :] - g)
    return jnp.einsum('bhlk,bhlv->bhkv', k.astype(F32) * w, v.astype(F32))


def gla_mixer(hc, hl, w_in, w_g1, w_g2, b_g, norm_w, w_out, ctx_out):
    def project(h):
        q, k, v, r = jnp.split(h @ w_in, [GLA_DK, 2 * GLA_DK, 2 * GLA_DK + GLA_DV], axis=-1)
        lgs = [to_heads(jax.nn.log_sigmoid(((h @ w_g1[d]) @ w_g2[d] + b_g[d]).astype(F32)) / GLA_TAU, GLA_HEADS)
               for d in range(2)]
        return (to_heads(q, GLA_HEADS) * GLA_HK ** -0.5, to_heads(k, GLA_HEADS),
                to_heads(v, GLA_HEADS), r, lgs)
    def finish(o, r):
        o = rmsnorm(o.astype(r.dtype), norm_w)
        return (from_heads(o) * jax.nn.silu(r)) @ w_out
    qc, kc, vc, rc, lgc = project(hc)
    ql, kl, vl, rl, lgl = project(hl)
    zero = jnp.zeros((hc.shape[0], GLA_HEADS, GLA_HK, GLA_HV), F32)
    oc, ol = 0.0, 0.0
    for d in range(2):
        f = (lambda t: t) if d == 0 else (lambda t: jnp.flip(t, axis=2))
        s_ctx = gla_final_state(f(kc), f(vc), f(lgc[d]))
        if ctx_out:
            oc = oc + f(gla_chunk_scan(f(qc), f(kc), f(vc), f(lgc[d]), zero))
        ol = ol + f(gla_chunk_scan(f(ql), f(kl), f(vl), f(lgl[d]), s_ctx))
    yc = finish(oc, rc) if ctx_out else None
    return yc, finish(ol, rl)


def mla_mixer(hc, hl, cos, sin, w_in, q_a_norm, w_uq, kv_a_norm, w_ukv, q_head_norm, k_head_norm, w_out, ctx_out):
    def project(h, rope):
        b, l, _ = h.shape
        cq, ckv, k_pe = jnp.split(h @ w_in, [MLA_Q_RANK, MLA_Q_RANK + MLA_KV_RANK], axis=-1)
        q = (rmsnorm(cq, q_a_norm) @ w_uq).reshape(b, l, MLA_HEADS, MLA_QK)
        kv = (rmsnorm(ckv, kv_a_norm) @ w_ukv).reshape(b, l, MLA_HEADS, MLA_NOPE + MLA_V)
        k_nope, v = jnp.split(kv, [MLA_NOPE], axis=-1)
        k_pe = jnp.broadcast_to(k_pe[:, :, None, :], (b, l, MLA_HEADS, MLA_ROPE))
        q = rmsnorm(q, q_head_norm)
        k = rmsnorm(jnp.concatenate([k_nope, k_pe], axis=-1), k_head_norm)
        if rope is not None:
            q, k = rope_tail(q, *rope), rope_tail(k, *rope)
        return q.transpose(0, 2, 1, 3), k.transpose(0, 2, 1, 3), v.transpose(0, 2, 1, 3)
    qc, kc, vc = project(hc, None)
    ql, kl, vl = project(hl, (cos, sin))
    k_all = jnp.concatenate([kc, kl], axis=2)
    v_all = jnp.concatenate([vc, vl], axis=2)
    yl = from_heads(block_attention(ql, k_all, v_all)) @ w_out
    yc = from_heads(block_attention(qc, kc, vc)) @ w_out if ctx_out else None
    return yc, yl


def ssd_chunk_scan(xs, dt, a, bm, cm, s0):
    b, l = xs.shape[:2]
    n = l // SSM_CHUNK
    rep = SSM_HEADS // SSM_GROUPS
    mask = jnp.tril(jnp.ones((SSM_CHUNK, SSM_CHUNK), bool))[None, :, :, None]
    def chunks(t):
        return jnp.moveaxis(t.reshape((b, n, SSM_CHUNK) + t.shape[2:]), 1, 0)
    def step(s, inp):
        xc, dc, bc, cc = inp
        bh = jnp.repeat(bc.astype(F32), rep, axis=2)
        ch = jnp.repeat(cc.astype(F32), rep, axis=2)
        cum = jnp.cumsum(dc * a, axis=1)
        seg = jnp.exp(jnp.where(mask, cum[:, :, None] - cum[:, None], -jnp.inf))
        xdt = xc.astype(F32) * dc[..., None]
        scores = jnp.einsum('bihn,bjhn->bijh', ch, bh) * seg
        y = jnp.einsum('bijh,bjhp->bihp', scores, xdt)
        y = y + jnp.einsum('bihn,bhpn->bihp', ch * jnp.exp(cum)[..., None], s)
        s = s * jnp.exp(cum[:, -1])[:, :, None, None] + jnp.einsum(
            'bjhn,bjhp->bhpn', bh * jnp.exp(cum[:, -1:] - cum)[..., None], xdt)
        return s, y
    _, y = lax.scan(step, s0, (chunks(xs), chunks(dt), chunks(bm), chunks(cm)))
    return jnp.moveaxis(y, 0, 1).reshape(xs.shape)


def ssd_final_state(xs, dt, a, bm):
    cum = jnp.cumsum(dt * a, axis=1)
    w = jnp.exp(cum[:, -1:] - cum) * dt
    bh = jnp.repeat(bm.astype(F32), SSM_HEADS // SSM_GROUPS, axis=2)
    return jnp.einsum('blhn,blhp->bhpn', bh * w[..., None], xs.astype(F32))


def ssd_mixer(hc, hl, w_in, conv_w, conv_b, dt_bias, a_log, d_skip, norm_w, w_out, ctx_out):
    a = -jnp.exp(a_log.astype(F32))
    def project(h):
        b, l, _ = h.shape
        z, xbc, dt = jnp.split(h @ w_in, [SSM_INNER, SSM_INNER + SSM_XBC], axis=-1)
        xbc = jax.nn.silu(dwconv_centred(xbc, conv_w, conv_b))
        xs, bm, cm = jnp.split(xbc, [SSM_INNER, SSM_INNER + SSM_BC], axis=-1)
        dt = jax.nn.softplus(dt.astype(F32).reshape(b, l, 2, SSM_HEADS) + dt_bias.astype(F32))
        return (z, xs.reshape(b, l, SSM_HEADS, SSM_HEADDIM),
                bm.reshape(b, l, SSM_GROUPS, SSM_STATE), cm.reshape(b, l, SSM_GROUPS, SSM_STATE), dt)
    def finish(y, xs, z):
        b, l = z.shape[:2]
        y = (y + xs.astype(F32) * d_skip.astype(F32)[:, None]).reshape(b, l, SSM_INNER).astype(z.dtype)
        y = (y * jax.nn.silu(z)).reshape(b, l, SSM_GROUPS, SSM_INNER // SSM_GROUPS)
        y = rmsnorm(y, norm_w.reshape(SSM_GROUPS, SSM_INNER // SSM_GROUPS)).reshape(b, l, SSM_INNER)
        return y @ w_out
    zc, xsc, bc, cc, dtc = project(hc)
    zl, xsl, bl, cl, dtl = project(hl)
    zero = jnp.zeros((hc.shape[0], SSM_HEADS, SSM_HEADDIM, SSM_STATE), F32)
    yc, yl = 0.0, 0.0
    for d in range(2):
        f = (lambda t: t) if d == 0 else (lambda t: jnp.flip(t, axis=1))
        s_ctx = ssd_final_state(f(xsc), f(dtc[:, :, d]), a[d], f(bc))
        if ctx_out:
            yc = yc + f(ssd_chunk_scan(f(xsc), f(dtc[:, :, d]), a[d], f(bc), f(cc), zero))
        yl = yl + f(ssd_chunk_scan(f(xsl), f(dtl[:, :, d]), a[d], f(bl), f(cl), s_ctx))
    out_c = finish(yc, xsc, zc) if ctx_out else None
    return out_c, finish(yl, xsl, zl)


def conv_ffn(h, w_up, conv_w, conv_b, w_down):
    gate, val = jnp.split(dwconv_centred(h @ w_up, conv_w, conv_b), 2, axis=-1)
    return (jax.nn.silu(gate) * val) @ w_down


def setup_inputs(seed: int = 0) -> dict:
    keys = iter(jax.random.split(jax.random.key(seed), 48))
    def nrm(shape, scale):
        return jax.random.normal(next(keys), shape, F32) * scale
    def gain(shape):
        return 1.0 + nrm(shape, 0.02)
    d, f2 = D_MODEL, 2 * FFN_HIDDEN
    dt = jnp.exp(jax.random.uniform(next(keys), (N_SSM, 2, SSM_HEADS), F32,
                                    float(np.log(1e-3)), float(np.log(1e-1))))
    return {
        'x': nrm((BATCH, SEQ, d), 1.0),
        'c': nrm((BATCH, d), 1.0),
        'ctx': nrm((BATCH, CTX_LEN, d), 1.0),
        'c_ctx': nrm((d,), 1.0),
        'ada_w': nrm((DEPTH, d, N_MOD * d), 0.5 * d ** -0.5),
        'ada_b': nrm((DEPTH, N_MOD * d), 0.02),
        'norm_mix_w': gain((DEPTH, d)),
        'norm_ffn_w': gain((DEPTH, d)),
        'ffn_w_up': nrm((DEPTH, d, f2), d ** -0.5),
        'ffn_conv_w': nrm((DEPTH, FFN_CONV, f2), FFN_CONV ** -0.5),
        'ffn_conv_b': nrm((DEPTH, f2), 0.02),
        'ffn_w_down': nrm((DEPTH, FFN_HIDDEN, d), FFN_HIDDEN ** -0.5),
        'gla_w_in': nrm((N_GLA, d, GLA_IN), d ** -0.5),
        'gla_w_g1': nrm((N_GLA, 2, d, GLA_GATE_RANK), d ** -0.5),
        'gla_w_g2': nrm((N_GLA, 2, GLA_GATE_RANK, GLA_DK), GLA_GATE_RANK ** -0.5),
        'gla_b_g': nrm((N_GLA, 2, GLA_DK), 0.1),
        'gla_norm_w': gain((N_GLA, GLA_HV)),
        'gla_w_out': nrm((N_GLA, GLA_DV, d), GLA_DV ** -0.5),
        'mla_w_in': nrm((N_MLA, d, MLA_IN), d ** -0.5),
        'mla_q_a_norm': gain((N_MLA, MLA_Q_RANK)),
        'mla_w_uq': nrm((N_MLA, MLA_Q_RANK, MLA_HEADS * MLA_QK), MLA_Q_RANK ** -0.5),
        'mla_kv_a_norm': gain((N_MLA, MLA_KV_RANK)),
        'mla_w_ukv': nrm((N_MLA, MLA_KV_RANK, MLA_HEADS * (MLA_NOPE + MLA_V)), MLA_KV_RANK ** -0.5),
        'mla_q_norm': gain((N_MLA, MLA_QK)),
        'mla_k_norm': gain((N_MLA, MLA_QK)),
        'mla_w_out': nrm((N_MLA, MLA_HEADS * MLA_V, d), (MLA_HEADS * MLA_V) ** -0.5),
        'ssm_w_in': nrm((N_SSM, d, SSM_IN), d ** -0.5),
        'ssm_conv_w': nrm((N_SSM, SSM_CONV, SSM_XBC), SSM_CONV ** -0.5),
        'ssm_conv_b': nrm((N_SSM, SSM_XBC), 0.02),
        'ssm_dt_bias': dt + jnp.log(-jnp.expm1(-dt)),
        'ssm_a_log': jnp.log(jax.random.uniform(next(keys), (N_SSM, 2, SSM_HEADS), F32, 1.0, 16.0)),
        'ssm_d': 1.0 + nrm((N_SSM, SSM_HEADS), 0.1),
        'ssm_norm_w': gain((N_SSM, SSM_INNER)),
        'ssm_w_out': nrm((N_SSM, SSM_INNER, d), SSM_INNER ** -0.5),
    }


def reference(x, c, ctx, c_ctx, ada_w, ada_b, norm_mix_w, norm_ffn_w, ffn_w_up, ffn_conv_w, ffn_conv_b,
              ffn_w_down, gla_w_in, gla_w_g1, gla_w_g2, gla_b_g, gla_norm_w, gla_w_out,
              mla_w_in, mla_q_a_norm, mla_w_uq, mla_kv_a_norm, mla_w_ukv, mla_q_norm, mla_k_norm, mla_w_out,
              ssm_w_in, ssm_conv_w, ssm_conv_b, ssm_dt_bias, ssm_a_log, ssm_d, ssm_norm_w, ssm_w_out):
    rows = x.shape[1] // GRID_W
    cos, sin = axial_rope(rows)
    xl, xc = x, ctx
    for i in range(DEPTH):
        kind, j = i % N_MIXERS, i // N_MIXERS
        ctx_out = i < DEPTH - 1
        sh1, sc1, g1, sh2, sc2, g2 = modulation(c, ada_w[i], ada_b[i])
        csh1, csc1, cg1, csh2, csc2, cg2 = modulation(c_ctx, ada_w[i], ada_b[i])
        hl = rmsnorm(xl, norm_mix_w[i]) * (1 + sc1) + sh1
        hc = rmsnorm(xc, norm_mix_w[i]) * (1 + csc1) + csh1
        if kind == 0:
            yc, yl = gla_mixer(hc, hl, gla_w_in[j], gla_w_g1[j], gla_w_g2[j], gla_b_g[j],
                               gla_norm_w[j], gla_w_out[j], ctx_out)
        elif kind == 1:
            yc, yl = mla_mixer(hc, hl, cos, sin, mla_w_in[j], mla_q_a_norm[j], mla_w_uq[j],
                               mla_kv_a_norm[j], mla_w_ukv[j], mla_q_norm[j], mla_k_norm[j],
                               mla_w_out[j], ctx_out)
        else:
            yc, yl = ssd_mixer(hc, hl, ssm_w_in[j], ssm_conv_w[j], ssm_conv_b[j], ssm_dt_bias[j],
                               ssm_a_log[j], ssm_d[j], ssm_norm_w[j], ssm_w_out[j], ctx_out)
        xl = xl + g1 * yl
        xl = xl + g2 * conv_ffn(rmsnorm(xl, norm_ffn_w[i]) * (1 + sc2) + sh2,
                                ffn_w_up[i], ffn_conv_w[i], ffn_conv_b[i], ffn_w_down[i])
        if ctx_out:
            xc = xc + cg1 * yc
            xc = xc + cg2 * conv_ffn(rmsnorm(xc, norm_ffn_w[i]) * (1 + csc2) + csh2,
                                     ffn_w_up[i], ffn_conv_w[i], ffn_conv_b[i], ffn_w_down[i])
    return xl
```

```python
import functools

import numpy as np
import jax
import jax.numpy as jnp
from jax import lax
from jax.experimental import pallas as pl
from jax.experimental.pallas import tpu as pltpu

F32 = jnp.float32
BF16 = jnp.bfloat16

D_MODEL = 1024
DEPTH = 4
GRID_W = 64
N_MIXERS = 3
N_MOD = 6
EPS = 1e-6

GLA_HEADS = 4
GLA_DK = D_MODEL // 2
GLA_DV = D_MODEL
GLA_HK = GLA_DK // GLA_HEADS
GLA_HV = GLA_DV // GLA_HEADS
GLA_GATE_RANK = 16
GLA_TAU = 16.0
GLA_IN = 2 * GLA_DK + 2 * GLA_DV

MLA_HEADS = 8
MLA_NOPE = 128
MLA_ROPE = 64
MLA_QK = MLA_NOPE + MLA_ROPE
MLA_V = 128
MLA_Q_RANK = 256
MLA_KV_RANK = 128
ROPE_THETA = 10000.0

SSM_INNER = 2 * D_MODEL
SSM_HEADDIM = 64
SSM_HEADS = SSM_INNER // SSM_HEADDIM
SSM_STATE = 128
SSM_GROUPS = 4
SSM_BC = SSM_GROUPS * SSM_STATE
SSM_XBC = SSM_INNER + 2 * SSM_BC

FFN_HIDDEN = 2816

LANES = 128
HALO = 8
VMEM_LIMIT_BYTES = 56 * 1024 * 1024
SCAN_CHUNK = 128
FFN_COL_CHUNK = 256
SSD_COL_CHUNK = 512


def _params(n_axes):
    return pltpu.CompilerParams(dimension_semantics=("arbitrary",) * n_axes,
                                vmem_limit_bytes=VMEM_LIMIT_BYTES)


def _dot(a, b):
    return jnp.dot(a, b, preferred_element_type=F32)


def _dot_nt(a, b):
    return lax.dot_general(a, b, (((1,), (1,)), ((), ())), preferred_element_type=F32)


def _dot_tn(a, b):
    return lax.dot_general(a, b, (((0,), (0,)), ((), ())), preferred_element_type=F32)


def _split_dot(mat_bf16, v_f32):
    hi = v_f32.astype(BF16)
    lo = (v_f32 - hi.astype(F32)).astype(BF16)
    return _dot(mat_bf16, hi) + _dot(mat_bf16, lo)


def _norm_mod(x, nw, sc, sh):
    ms = jnp.mean(x * x, axis=-1, keepdims=True)
    return (x * lax.rsqrt(ms + EPS) * nw) * (1.0 + sc) + sh


def _silu(x):
    return x * jax.nn.sigmoid(x)


def _mod_index(n_mod):
    if n_mod == 1:
        return lambda b, i: (0, 0, 0)
    return lambda b, i: (b, 0, 0)


def _modulation_kernel(cond_ref, w_ref, b_ref, o_ref):
    a = _silu(cond_ref[...]).astype(BF16)
    o_ref[0] = _dot(a, w_ref[0].astype(BF16)) + b_ref[0]


def _modulation(cond, ada_w, ada_b):
    r, d = cond.shape
    depth, _, n = ada_w.shape
    tn = 512
    return pl.pallas_call(
        _modulation_kernel,
        grid=(depth, n // tn),
        in_specs=[pl.BlockSpec((r, d), lambda l, j: (0, 0)),
                  pl.BlockSpec((1, d, tn), lambda l, j: (l, 0, j)),
                  pl.BlockSpec((1, 1, tn), lambda l, j: (l, 0, j))],
        out_specs=pl.BlockSpec((1, r, tn), lambda l, j: (l, 0, j)),
        out_shape=jax.ShapeDtypeStruct((depth, r, n), F32),
        compiler_params=_params(2),
        name="adaln_modulation",
    )(cond, ada_w, ada_b.reshape(depth, 1, n))


def _halo_specs(tm, l, d):
    per = tm // HALO
    last = l // HALO - 1
    return [pl.BlockSpec((1, tm, d), lambda b, i: (b, i, 0)),
            pl.BlockSpec((1, HALO, d), lambda b, i: (b, jnp.maximum(i * per - 1, 0), 0)),
            pl.BlockSpec((1, HALO, d), lambda b, i: (b, jnp.minimum((i + 1) * per, last), 0))]


def _halo_rows(x_ref, xp_ref, xn_ref, nw, sc, sh):
    i = pl.program_id(1)
    has_prev = (i > 0).astype(F32)
    has_next = (i < pl.num_programs(1) - 1).astype(F32)
    hp = _norm_mod(xp_ref[0], nw, sc, sh) * has_prev
    hm = _norm_mod(x_ref[0], nw, sc, sh)
    hn = _norm_mod(xn_ref[0], nw, sc, sh) * has_next
    return jnp.concatenate([hp, hm, hn], axis=0).astype(BF16)


def _conv3(u_ref, tm, w, b):
    return (u_ref[pl.ds(HALO - 1, tm), :] * w[0:1] + u_ref[pl.ds(HALO, tm), :] * w[1:2]
            + u_ref[pl.ds(HALO + 1, tm), :] * w[2:3] + b)


def _ffn_kernel(x_ref, xp_ref, xn_ref, sc_ref, sh_ref, g_ref, nw_ref, wg_ref, wv_ref, cwg_ref, cwv_ref,
                cbg_ref, cbv_ref, wd_ref, o_ref, h_buf, ug_buf, uv_buf, *, tm, n_chunks):
    h_buf[...] = _halo_rows(x_ref, xp_ref, xn_ref, nw_ref[...], sc_ref[0], sh_ref[0])
    acc = jnp.zeros((tm, D_MODEL), F32)
    for c in range(n_chunks):
        ug_buf[...] = _dot(h_buf[...], wg_ref[c])
        uv_buf[...] = _dot(h_buf[...], wv_ref[c])
        gate = _conv3(ug_buf, tm, cwg_ref[c], cbg_ref[c])
        val = _conv3(uv_buf, tm, cwv_ref[c], cbv_ref[c])
        acc = acc + _dot((_silu(gate) * val).astype(BF16), wd_ref[c])
    o_ref[0] = x_ref[0] + g_ref[0] * acc


def _conv_ffn_residual(x, sc, sh, g, nw, wts, tm):
    b, l, d = x.shape
    wg, wv, cwg, cwv, cbg, cbv, wd = wts
    n_chunks, _, fc = wg.shape
    mod = pl.BlockSpec((1, 1, d), _mod_index(sc.shape[0]))
    full3 = lambda a: pl.BlockSpec(a.shape, lambda b_, i: (0, 0, 0))
    kern = functools.partial(_ffn_kernel, tm=tm, n_chunks=n_chunks)
    return pl.pallas_call(
        kern,
        grid=(b, l // tm),
        in_specs=_halo_specs(tm, l, d) + [mod, mod, mod, pl.BlockSpec((1, d), lambda b_, i: (0, 0)),
                                           full3(wg), full3(wv), full3(cwg), full3(cwv), full3(cbg), full3(cbv),
                                           full3(wd)],
        out_specs=pl.BlockSpec((1, tm, d), lambda b_, i: (b_, i, 0)),
        out_shape=jax.ShapeDtypeStruct((b, l, d), F32),
        scratch_shapes=[pltpu.VMEM((tm + 2 * HALO, d), BF16),
                        pltpu.VMEM((tm + 2 * HALO, fc), F32),
                        pltpu.VMEM((tm + 2 * HALO, fc), F32)],
        compiler_params=_params(2),
        name="conv_ffn",
    )(x, x, x, sc, sh, g, nw, wg, wv, cwg, cwv, cbg, cbv, wd)


def _ffn_weights(w_up, conv_w, conv_b, w_down):
    f, fc = FFN_HIDDEN, FFN_COL_CHUNK
    n = f // fc
    cols = lambda w: jnp.transpose(w.reshape(w.shape[0], n, fc), (1, 0, 2))
    return (cols(w_up[:, :f]).astype(BF16), cols(w_up[:, f:]).astype(BF16),
            cols(conv_w[:, :f]), cols(conv_w[:, f:]),
            cols(conv_b[None, :f]), cols(conv_b[None, f:]),
            w_down.reshape(n, fc, D_MODEL).astype(BF16))


def _out_proj_kernel(a_ref, x_ref, g_ref, w_ref, o_ref):
    o_ref[0] = x_ref[0] + g_ref[0] * _dot(a_ref[0].astype(BF16), w_ref[...])


def _out_proj_residual(a, x, g, w, tm):
    b, l, d = x.shape
    k = a.shape[-1]
    return pl.pallas_call(
        _out_proj_kernel,
        grid=(b, l // tm),
        in_specs=[pl.BlockSpec((1, tm, k), lambda b_, i: (b_, i, 0)),
                  pl.BlockSpec((1, tm, d), lambda b_, i: (b_, i, 0)),
                  pl.BlockSpec((1, 1, d), _mod_index(g.shape[0])),
                  pl.BlockSpec((k, d), lambda b_, i: (0, 0))],
        out_specs=pl.BlockSpec((1, tm, d), lambda b_, i: (b_, i, 0)),
        out_shape=jax.ShapeDtypeStruct((b, l, d), F32),
        compiler_params=_params(2),
        name="out_proj_residual",
    )(a, x, g, w)


GLA_W1_COLS = GLA_IN + LANES


def _gla_proj_kernel(x_ref, sc_ref, sh_ref, nw_ref, w1_ref, w2_ref, bg_ref, p_ref, lg_ref):
    h = _norm_mod(x_ref[0], nw_ref[...], sc_ref[0], sh_ref[0]).astype(BF16)
    p = _dot(h, w1_ref[...])
    p_ref[0] = p[:, :GLA_IN]
    z = _dot(p[:, GLA_IN:].astype(BF16), w2_ref[...]) + bg_ref[...]
    lg_ref[0] = jax.nn.log_sigmoid(z) * (1.0 / GLA_TAU)


def _gla_project(x, sc, sh, nw, w1, w2, bg, tm):
    b, l, d = x.shape
    mod = pl.BlockSpec((1, 1, d), _mod_index(sc.shape[0]))
    const = lambda a: pl.BlockSpec(a.shape, lambda b_, i: (0, 0))
    return pl.pallas_call(
        _gla_proj_kernel,
        grid=(b, l // tm),
        in_specs=[pl.BlockSpec((1, tm, d), lambda b_, i: (b_, i, 0)), mod, mod, const(nw), const(w1), const(w2),
                  const(bg)],
        out_specs=[pl.BlockSpec((1, tm, GLA_IN), lambda b_, i: (b_, i, 0)),
                   pl.BlockSpec((1, tm, 2 * GLA_DK), lambda b_, i: (b_, i, 0))],
        out_shape=[jax.ShapeDtypeStruct((b, l, GLA_IN), F32), jax.ShapeDtypeStruct((b, l, 2 * GLA_DK), F32)],
        compiler_params=_params(2),
        name="gla_in_proj",
    )(x, sc, sh, nw, w1, w2, bg)


def _gla_tables(c):
    n_lev = int(np.log2(c))
    idx = np.arange(c)
    i, t = idx[:, None], idx[None, :]
    dec = np.zeros(((n_lev + 2) * c + 8, c), np.float32)
    dec[0:c] = t <= i
    dec[c:2 * c] = t > i
    mask = np.zeros((n_lev + 1, c, c), np.float32)
    mask[0] = np.eye(c)
    for lev in range(n_lev):
        m = c >> (lev + 1)
        mid = (i // (2 * m)) * 2 * m + m
        upper = (i % (2 * m)) >= m
        dec[(2 + lev) * c:(3 + lev) * c] = np.where(upper, (t >= mid) & (t <= i), (t > i) & (t < mid))
        j = t
        mask[lev + 1] = (i // (2 * m) == j // (2 * m)) & upper & ((j % (2 * m)) < m)
    dec[(n_lev + 2) * c:] = 1.0
    rev = lambda a: a[..., ::-1, ::-1]
    dec_b = np.concatenate([np.concatenate([rev(dec[k * c:(k + 1) * c]) for k in range(n_lev + 2)], 0),
                            dec[(n_lev + 2) * c:]], 0)
    return (jnp.asarray(np.stack([dec, dec_b]), BF16), jnp.asarray(np.stack([mask, rev(mask)]), F32))


def _gla_scan_kernel(q_ref, k_ref, v_ref, lg_ref, dec_ref, mask_ref, s0_ref, o_ref, sT_ref, state, *, c, n_lev):
    @pl.when(pl.program_id(3) == 0)
    def _():
        state[...] = s0_ref[0, 0, 0]

    q = q_ref[0] * (GLA_HK ** -0.5)
    k = k_ref[0]
    vb = v_ref[0].astype(BF16)
    e_all = jnp.exp(_split_dot(dec_ref[0], lg_ref[0]))
    att = mask_ref[0, 0] * _dot_nt(q.astype(BF16), k.astype(BF16))
    for lev in range(n_lev):
        e = e_all[(2 + lev) * c:(3 + lev) * c]
        att = att + mask_ref[0, lev + 1] * _dot_nt((q * e).astype(BF16), (k * e).astype(BF16))
    st = state[...]
    o_ref[0, 0] = _dot(att.astype(BF16), vb) + _dot_nt((q * e_all[0:c]).astype(BF16), st.astype(BF16))
    e_tot = e_all[(n_lev + 2) * c:(n_lev + 2) * c + 1]
    st = st * e_tot + _dot_tn(vb, (k * e_all[c:2 * c]).astype(BF16))
    state[...] = st
    sT_ref[0, 0, 0] = st


def _gla_scan(proj, lg, tables, s0):
    b, l, _ = proj.shape
    c = SCAN_CHUNK
    n = l // c
    dec, mask = tables
    n_lev = mask.shape[1] - 1
    chunk = lambda d, i: i + d * (n - 1 - 2 * i)
    kb, vb0 = GLA_DK // GLA_HK, 2 * GLA_DK // GLA_HV
    kern = functools.partial(_gla_scan_kernel, c=c, n_lev=n_lev)
    return pl.pallas_call(
        kern,
        grid=(2, b, GLA_HEADS, n),
        in_specs=[pl.BlockSpec((1, c, GLA_HK), lambda d, b_, h, i: (b_, chunk(d, i), h)),
                  pl.BlockSpec((1, c, GLA_HK), lambda d, b_, h, i: (b_, chunk(d, i), kb + h)),
                  pl.BlockSpec((1, c, GLA_HV), lambda d, b_, h, i: (b_, chunk(d, i), vb0 + h)),
                  pl.BlockSpec((1, c, GLA_HK), lambda d, b_, h, i: (b_, chunk(d, i), d * GLA_HEADS + h)),
                  pl.BlockSpec((1,) + dec.shape[1:], lambda d, b_, h, i: (d, 0, 0)),
                  pl.BlockSpec((1,) + mask.shape[1:], lambda d, b_, h, i: (d, 0, 0, 0)),
                  pl.BlockSpec((1, 1, 1, GLA_HV, GLA_HK), lambda d, b_, h, i: (d, b_, h, 0, 0))],
        out_specs=[pl.BlockSpec((1, 1, c, GLA_HV), lambda d, b_, h, i: (d, b_, chunk(d, i), h)),
                   pl.BlockSpec((1, 1, 1, GLA_HV, GLA_HK), lambda d, b_, h, i: (d, b_, h, 0, 0))],
        out_shape=[jax.ShapeDtypeStruct((2, b, l, GLA_DV), F32),
                   jax.ShapeDtypeStruct((2, b, GLA_HEADS, GLA_HV, GLA_HK), F32)],
        scratch_shapes=[pltpu.VMEM((GLA_HV, GLA_HK), F32)],
        compiler_params=_params(4),
        name="gla_scan",
    )(proj, proj, proj, lg, dec, mask, s0)


def _gla_finish_kernel(of_ref, ob_ref, r_ref, x_ref, g_ref, nw_ref, w_ref, o_ref):
    o = of_ref[0, 0] + ob_ref[0, 0]
    heads = []
    for h in range(GLA_HEADS):
        oh = o[:, h * GLA_HV:(h + 1) * GLA_HV]
        ms = jnp.mean(oh * oh, axis=-1, keepdims=True)
        heads.append(oh * lax.rsqrt(ms + EPS) * nw_ref[...])
    y = jnp.concatenate(heads, axis=-1) * _silu(r_ref[0])
    o_ref[0] = x_ref[0] + g_ref[0] * _dot(y.astype(BF16), w_ref[...])


def _gla_finish(o2, proj, x, g, nw, w, tm):
    b, l, d = x.shape
    rb = (2 * GLA_DK + GLA_DV) // GLA_DV
    return pl.pallas_call(
        _gla_finish_kernel,
        grid=(b, l // tm),
        in_specs=[pl.BlockSpec((1, 1, tm, GLA_DV), lambda b_, i: (0, b_, i, 0)),
                  pl.BlockSpec((1, 1, tm, GLA_DV), lambda b_, i: (1, b_, i, 0)),
                  pl.BlockSpec((1, tm, GLA_DV), lambda b_, i: (b_, i, rb)),
                  pl.BlockSpec((1, tm, d), lambda b_, i: (b_, i, 0)),
                  pl.BlockSpec((1, 1, d), _mod_index(g.shape[0])),
                  pl.BlockSpec((1, GLA_HV), lambda b_, i: (0, 0)),
                  pl.BlockSpec((GLA_DV, d), lambda b_, i: (0, 0))],
        out_specs=pl.BlockSpec((1, tm, d), lambda b_, i: (b_, i, 0)),
        out_shape=jax.ShapeDtypeStruct((b, l, d), F32),
        compiler_params=_params(2),
        name="gla_finish",
    )(o2, o2, proj, x, g, nw, w)


def _gla_weights(w_in, w_g1, w_g2, b_g, norm_w, w_out):
    d = w_in.shape[0]
    r = GLA_GATE_RANK
    w1 = jnp.concatenate([w_in, w_g1[0], w_g1[1], jnp.zeros((d, LANES - 2 * r), F32)], axis=1).astype(BF16)
    w2 = jnp.zeros((LANES, 2 * GLA_DK), F32)
    w2 = w2.at[0:r, :GLA_DK].set(w_g2[0]).at[r:2 * r, GLA_DK:].set(w_g2[1]).astype(BF16)
    return w1, w2, b_g.reshape(1, 2 * GLA_DK), norm_w.reshape(1, GLA_HV), w_out.astype(BF16)


def _gla_mixer(xc, xl, mod_c, mod_l, nw, wts, tables, ctx_out, tm_l, tm_c):
    w1, w2, bg, norm_w, w_out = wts
    (csc, csh, cg), (sc, sh, g) = mod_c, mod_l
    pc, lgc = _gla_project(xc, csc, csh, nw, w1, w2, bg, tm_c)
    plat, lgl = _gla_project(xl, sc, sh, nw, w1, w2, bg, tm_l)
    zero = jnp.zeros((2, xc.shape[0], GLA_HEADS, GLA_HV, GLA_HK), F32)
    oc, s_ctx = _gla_scan(pc, lgc, tables, zero)
    ol, _ = _gla_scan(plat, lgl, tables, s_ctx)
    xl = _gla_finish(ol, plat, xl, g, norm_w, w_out, tm_l)
    if ctx_out:
        xc = _gla_finish(oc, pc, xc, cg, norm_w, w_out, tm_c)
    return xc, xl


MLA_HEAD_PAD = 2 * LANES
MLA_IN_COLS = MLA_Q_RANK + MLA_KV_RANK + 2 * MLA_ROPE


def _rot_cols(w):
    half = MLA_ROPE // 2
    return jnp.concatenate([-w[..., half:], w[..., :half]], axis=-1)


def _swap_halves(w):
    half = MLA_ROPE // 2
    return jnp.concatenate([w[..., half:], w[..., :half]], axis=-1)


def _mla_proj_kernel(x_ref, sc_ref, sh_ref, nw_ref, win_ref, qan_ref, kvan_ref, wuq_ref, wukv_ref,
                     qna_ref, qnb_ref, kna_ref, knb_ref, cs_ref, q_ref, k_ref, v_ref):
    h = _norm_mod(x_ref[0], nw_ref[...], sc_ref[0], sh_ref[0]).astype(BF16)
    p = _dot(h, win_ref[...])
    cq = p[:, :MLA_Q_RANK]
    ckv = p[:, MLA_Q_RANK:MLA_Q_RANK + MLA_KV_RANK]
    kpe = p[:, MLA_Q_RANK + MLA_KV_RANK:]
    cq = cq * lax.rsqrt(jnp.mean(cq * cq, axis=-1, keepdims=True) + EPS) * qan_ref[...]
    ckv = ckv * lax.rsqrt(jnp.mean(ckv * ckv, axis=-1, keepdims=True) + EPS) * kvan_ref[...]
    qall = _dot(cq.astype(BF16), wuq_ref[...])
    kvall = _dot(ckv.astype(BF16), wukv_ref[...])
    cs = cs_ref[...]
    lane = lax.broadcasted_iota(jnp.int32, (1, LANES), 1)
    keep = (lane < MLA_ROPE).astype(F32)
    kpe_ss = 0.5 * jnp.sum(kpe * kpe, axis=-1, keepdims=True)
    scale = MLA_QK ** -0.5

    def head(nope, pe, na, nb, mult):
        ss = jnp.sum(nope * nope, axis=-1, keepdims=True) + 0.5 * jnp.sum(pe * pe, axis=-1, keepdims=True)
        s = lax.rsqrt(ss * (1.0 / MLA_QK) + EPS)
        t = pe * s * nb * cs
        rope = (t + pltpu.roll(t, MLA_ROPE, 1)) * keep
        return jnp.concatenate([nope * s * na * mult, rope * mult], axis=-1).astype(BF16)

    for hd in range(MLA_HEADS):
        qh = qall[:, hd * MLA_HEAD_PAD:(hd + 1) * MLA_HEAD_PAD]
        q_ref[0, hd] = head(qh[:, :LANES], qh[:, LANES:], qna_ref[...], qnb_ref[...], scale)
        kh = kvall[:, hd * 2 * LANES:hd * 2 * LANES + LANES]
        ss = jnp.sum(kh * kh, axis=-1, keepdims=True) + kpe_ss
        s = lax.rsqrt(ss * (1.0 / MLA_QK) + EPS)
        t = kpe * s * knb_ref[...] * cs
        rope = (t + pltpu.roll(t, MLA_ROPE, 1)) * keep
        k_ref[0, hd] = jnp.concatenate([kh * s * kna_ref[...], rope], axis=-1).astype(BF16)
        v_ref[0, hd] = kvall[:, hd * 2 * LANES + LANES:(hd + 1) * 2 * LANES].astype(BF16)


def _mla_project(x, sc, sh, nw, wts, cs, tm):
    b, l, d = x.shape
    win, qan, kvan, wuq, wukv, qna, qnb, kna, knb = wts[:9]
    mod = pl.BlockSpec((1, 1, d), _mod_index(sc.shape[0]))
    const = lambda a: pl.BlockSpec(a.shape, lambda b_, i: (0, 0))
    cs_spec = (pl.BlockSpec((tm, LANES), lambda b_, i: (i, 0)) if cs.shape[0] == l
               else pl.BlockSpec((tm, LANES), lambda b_, i: (0, 0)))
    hspec = lambda w: pl.BlockSpec((1, MLA_HEADS, tm, w), lambda b_, i: (b_, 0, i, 0))
    return pl.pallas_call(
        _mla_proj_kernel,
        grid=(b, l // tm),
        in_specs=[pl.BlockSpec((1, tm, d), lambda b_, i: (b_, i, 0)), mod, mod, const(nw), const(win), const(qan),
                  const(kvan), const(wuq), const(wukv), const(qna), const(qnb), const(kna), const(knb), cs_spec],
        out_specs=[hspec(MLA_HEAD_PAD), hspec(MLA_HEAD_PAD), hspec(MLA_V)],
        out_shape=[jax.ShapeDtypeStruct((b, MLA_HEADS, l, MLA_HEAD_PAD), BF16),
                   jax.ShapeDtypeStruct((b, MLA_HEADS, l, MLA_HEAD_PAD), BF16),
                   jax.ShapeDtypeStruct((b, MLA_HEADS, l, MLA_V), BF16)],
        compiler_params=_params(2),
        name="mla_project",
    )(x, sc, sh, nw, win, qan, kvan, wuq, wukv, qna, qnb, kna, knb, cs)


def _flash_kernel(*refs, tk, n_kv, has_ctx):
    if has_ctx:
        q_ref, k_ref, v_ref, kc_ref, vc_ref, o_ref = refs
    else:
        q_ref, k_ref, v_ref, o_ref = refs
    q = q_ref[0, 0]

    def update(carry, kj, vj):
        m, l, acc = carry
        s = _dot_nt(q, kj)
        m_new = jnp.maximum(m, jnp.max(s, axis=-1, keepdims=True))
        alpha = jnp.exp(m - m_new)
        p = jnp.exp(s - m_new)
        return (m_new, alpha * l + jnp.sum(p, axis=-1, keepdims=True),
                alpha * acc + _dot(p.astype(BF16), vj))

    tq = q.shape[0]
    carry = (jnp.full((tq, 1), -jnp.inf, F32), jnp.zeros((tq, 1), F32), jnp.zeros((tq, MLA_V), F32))
    if has_ctx:
        carry = update(carry, kc_ref[0, 0], vc_ref[0, 0])

    def body(j, carry):
        start = pl.multiple_of(j * tk, tk)
        return update(carry, k_ref[0, 0, pl.ds(start, tk), :], v_ref[0, 0, pl.ds(start, tk), :])

    m, l, acc = lax.fori_loop(0, n_kv, body, carry)
    o_ref[0] = (acc / l).astype(o_ref.dtype)


def _flash_attention(q, k, v, kc, vc, tq, tk):
    b, hds, l, dq = q.shape
    lk = k.shape[2]
    has_ctx = kc is not None
    kern = functools.partial(_flash_kernel, tk=tk, n_kv=lk // tk, has_ctx=has_ctx)
    whole = lambda a: pl.BlockSpec((1, 1) + a.shape[2:], lambda b_, h, i: (b_, h, 0, 0))
    in_specs = [pl.BlockSpec((1, 1, tq, dq), lambda b_, h, i: (b_, h, i, 0)), whole(k), whole(v)]
    args = [q, k, v]
    if has_ctx:
        in_specs += [whole(kc), whole(vc)]
        args += [kc, vc]
    return pl.pallas_call(
        kern,
        grid=(b, hds, l // tq),
        in_specs=in_specs,
        out_specs=pl.BlockSpec((1, tq, MLA_V), lambda b_, h, i: (b_, i, h)),
        out_shape=jax.ShapeDtypeStruct((b, l, hds * MLA_V), BF16),
        compiler_params=_params(3),
        name="mla_flash_attention",
    )(*args)


def _mla_weights(w_in, q_a_norm, w_uq, kv_a_norm, w_ukv, q_norm, k_norm, w_out):
    kpe = w_in[:, MLA_Q_RANK + MLA_KV_RANK:]
    win = jnp.concatenate([w_in, _rot_cols(kpe)], axis=1).astype(BF16)
    wq = w_uq.reshape(MLA_Q_RANK, MLA_HEADS, MLA_QK)
    wq = jnp.concatenate([wq, _rot_cols(wq[..., MLA_NOPE:])], axis=-1)
    wuq = wq.reshape(MLA_Q_RANK, MLA_HEADS * MLA_HEAD_PAD).astype(BF16)
    gains = lambda nrm: (nrm[None, :MLA_NOPE],
                         jnp.concatenate([nrm[MLA_NOPE:], _swap_halves(nrm[MLA_NOPE:])])[None])
    qna, qnb = gains(q_norm)
    kna, knb = gains(k_norm)
    return (win, q_a_norm[None], kv_a_norm[None], wuq, w_ukv.astype(BF16), qna, qnb, kna, knb, w_out.astype(BF16))


def _rope_table(l):
    rows = l // GRID_W
    pos_r = jnp.repeat(jnp.arange(rows, dtype=F32), GRID_W)
    pos_c = jnp.tile(jnp.arange(GRID_W, dtype=F32), rows)
    n_freq = MLA_ROPE // 4
    inv = ROPE_THETA ** (-jnp.arange(n_freq, dtype=F32) / n_freq)
    ang = jnp.concatenate([pos_r[:, None] * inv, pos_c[:, None] * inv], axis=-1)
    return jnp.concatenate([jnp.cos(ang), jnp.cos(ang), jnp.sin(ang), jnp.sin(ang)], axis=-1)


def _mla_mixer(xc, xl, mod_c, mod_l, nw, wts, ctx_out, tm_l, tm_c):
    (csc, csh, cg), (sc, sh, g) = mod_c, mod_l
    w_out = wts[9]
    lc, l = xc.shape[1], xl.shape[1]
    no_rope = jnp.concatenate([jnp.ones((tm_c, LANES // 2), F32), jnp.zeros((tm_c, LANES // 2), F32)], axis=-1)
    qc, kc, vc = _mla_project(xc, csc, csh, nw, wts, no_rope, tm_c)
    ql, kl, vl = _mla_project(xl, sc, sh, nw, wts, _rope_table(l), tm_l)
    tq = min(512, l)
    al = _flash_attention(ql, kl, vl, kc, vc, tq, min(512, l))
    xl = _out_proj_residual(al, xl, g, w_out, tm_l)
    if ctx_out:
        ac = _flash_attention(qc, kc, vc, None, None, lc, lc)
        xc = _out_proj_residual(ac, xc, cg, w_out, tm_c)
    return xc, xl


def _ssd_proj_kernel(x_ref, xp_ref, xn_ref, sc_ref, sh_ref, nw_ref, wz_ref, wx_ref, wdt_ref, cw_ref, cb_ref,
                     dtb_ref, z_ref, xbc_ref, dt_ref, h_buf, u_buf, *, tm, n_chunks, cc):
    h_buf[...] = _halo_rows(x_ref, xp_ref, xn_ref, nw_ref[...], sc_ref[0], sh_ref[0])
    hm = h_buf[pl.ds(HALO, tm), :]
    z_ref[0] = _dot(hm, wz_ref[...])
    dt = jax.nn.softplus(_dot(hm, wdt_ref[...]) + dtb_ref[...])
    dt_ref[0, 0] = dt[:, :LANES]
    dt_ref[1, 0] = dt[:, LANES:]
    for c in range(n_chunks):
        u_buf[...] = _dot(h_buf[...], wx_ref[c])
        xbc_ref[0, :, c * cc:(c + 1) * cc] = _silu(_conv3(u_buf, tm, cw_ref[c], cb_ref[c]))


def _ssd_project(x, sc, sh, nw, wts, tm):
    b, l, d = x.shape
    wz, wx, wdt, cw, cb, dtb = wts[:6]
    n_chunks, _, cc = wx.shape
    mod = pl.BlockSpec((1, 1, d), _mod_index(sc.shape[0]))
    const2 = lambda a: pl.BlockSpec(a.shape, lambda b_, i: (0, 0))
    const3 = lambda a: pl.BlockSpec(a.shape, lambda b_, i: (0, 0, 0))
    kern = functools.partial(_ssd_proj_kernel, tm=tm, n_chunks=n_chunks, cc=cc)
    return pl.pallas_call(
        kern,
        grid=(b, l // tm),
        in_specs=_halo_specs(tm, l, d) + [mod, mod, const2(nw), const2(wz), const3(wx), const2(wdt), const3(cw),
                                           const3(cb), const2(dtb)],
        out_specs=[pl.BlockSpec((1, tm, SSM_INNER), lambda b_, i: (b_, i, 0)),
                   pl.BlockSpec((1, tm, SSM_XBC), lambda b_, i: (b_, i, 0)),
                   pl.BlockSpec((2, 1, tm, LANES), lambda b_, i: (0, b_, i, 0))],
        out_shape=[jax.ShapeDtypeStruct((b, l, SSM_INNER), F32),
                   jax.ShapeDtypeStruct((b, l, SSM_XBC), F32),
                   jax.ShapeDtypeStruct((2, b, l, LANES), F32)],
        scratch_shapes=[pltpu.VMEM((tm + 2 * HALO, d), BF16), pltpu.VMEM((tm + 2 * HALO, cc), F32)],
        compiler_params=_params(2),
        name="ssd_in_proj_conv",
    )(x, x, x, sc, sh, nw, wz, wx, wdt, cw, cb, dtb)


def _ssd_tables(c):
    idx = np.arange(c)
    i, t = idx[:, None], idx[None, :]
    ones = np.ones((8, c), np.float32)
    fwd = np.concatenate([(t <= i).astype(np.float32), ones], 0)
    bwd = np.concatenate([(t >= i).astype(np.float32), ones], 0)
    mask = np.stack([(t <= i), (t >= i)]).astype(np.float32)
    return jnp.asarray(np.stack([fwd, bwd]), BF16), jnp.asarray(mask, F32)


def _ssd_scan_kernel(x_ref, b_ref, c_ref, dt_ref, a_ref, tri_ref, mask_ref, s0_ref, y_ref, sf_ref, state, *, c):
    @pl.when(pl.program_id(2) == 0)
    def _():
        state[...] = s0_ref[0, 0]

    dt = dt_ref[0, 0]
    cums = _split_dot(tri_ref[0], dt * a_ref[0])
    cum = cums[:c]
    total = cums[c:c + 1]
    e_in = jnp.exp(cum)
    e_out = jnp.exp(total - cum)
    e_tot = jnp.exp(total)
    cum_t = jnp.transpose(cum)
    causal = mask_ref[0] > 0.5
    lane = lax.broadcasted_iota(jnp.int32, (1, LANES), 1)
    left = lane < SSM_HEADDIM
    leftf = left.astype(F32)
    per_group = SSM_HEADS // SSM_GROUPS
    gw = per_group * SSM_HEADDIM

    for g in range(SSM_GROUPS):
        bm = b_ref[0, :, g * SSM_STATE:(g + 1) * SSM_STATE].astype(BF16)
        cm = c_ref[0, :, g * SSM_STATE:(g + 1) * SSM_STATE].astype(BF16)
        scores = _dot_nt(cm, bm)
        sg = state[g * gw:(g + 1) * gw, :]
        y_in = _dot_nt(cm, sg.astype(BF16))
        wx_parts = []
        for pr in range(per_group // 2):
            h0 = g * per_group + 2 * pr
            col = (g * per_group // 2 + pr) * LANES
            pair = lambda v: jnp.where(left, v[:, h0:h0 + 1], v[:, h0 + 1:h0 + 2])
            xdt = x_ref[0, :, col:col + LANES] * pair(dt)
            ms = []
            for h in (h0, h0 + 1):
                seg = jnp.exp(cum[:, h:h + 1] - cum_t[h:h + 1, :])
                ms.append((scores * jnp.where(causal, seg, 0.0)).astype(BF16))
            xbd = jnp.concatenate([xdt * leftf, xdt * (1.0 - leftf)], axis=0).astype(BF16)
            y = _dot(jnp.concatenate(ms, axis=1), xbd)
            y = y + pair(e_in) * y_in[:, 2 * pr * SSM_HEADDIM:(2 * pr + 2) * SSM_HEADDIM]
            y_ref[0, 0, :, col:col + LANES] = y
            wx_parts.append((xdt * pair(e_out)).astype(BF16))
        upd = _dot_tn(jnp.concatenate(wx_parts, axis=1), bm)
        dec = jnp.concatenate(
            [jnp.broadcast_to(e_tot[:, g * per_group + j:g * per_group + j + 1], (SSM_HEADDIM, SSM_STATE))
             for j in range(per_group)], axis=0)
        state[g * gw:(g + 1) * gw, :] = sg * dec + upd
    sf_ref[0, 0] = state[...]


def _ssd_scan(xbc, dt2, a_rows, tables, s0):
    b, l, _ = xbc.shape
    c = SCAN_CHUNK
    n = l // c
    tri, mask = tables
    chunk = lambda d, i: i + d * (n - 1 - 2 * i)
    xb = SSM_INNER // SSM_BC
    kern = functools.partial(_ssd_scan_kernel, c=c)
    return pl.pallas_call(
        kern,
        grid=(2, b, n),
        in_specs=[pl.BlockSpec((1, c, SSM_INNER), lambda d, b_, i: (b_, chunk(d, i), 0)),
                  pl.BlockSpec((1, c, SSM_BC), lambda d, b_, i: (b_, chunk(d, i), xb)),
                  pl.BlockSpec((1, c, SSM_BC), lambda d, b_, i: (b_, chunk(d, i), xb + 1)),
                  pl.BlockSpec((1, 1, c, LANES), lambda d, b_, i: (d, b_, chunk(d, i), 0)),
                  pl.BlockSpec((1, 1, LANES), lambda d, b_, i: (d, 0, 0)),
                  pl.BlockSpec((1, c + 8, c), lambda d, b_, i: (d, 0, 0)),
                  pl.BlockSpec((1, c, c), lambda d, b_, i: (d, 0, 0)),
                  pl.BlockSpec((1, 1, SSM_INNER, SSM_STATE), lambda d, b_, i: (d, b_, 0, 0))],
        out_specs=[pl.BlockSpec((1, 1, c, SSM_INNER), lambda d, b_, i: (d, b_, chunk(d, i), 0)),
                   pl.BlockSpec((1, 1, SSM_INNER, SSM_STATE), lambda d, b_, i: (d, b_, 0, 0))],
        out_shape=[jax.ShapeDtypeStruct((2, b, l, SSM_INNER), F32),
                   jax.ShapeDtypeStruct((2, b, SSM_INNER, SSM_STATE), F32)],
        scratch_shapes=[pltpu.VMEM((SSM_INNER, SSM_STATE), F32)],
        compiler_params=_params(3),
        name="ssd_scan",
    )(xbc, xbc, xbc, dt2, a_rows, tri, mask, s0)


def _ssd_finish_kernel(yf_ref, yb_ref, xs_ref, z_ref, x_ref, g_ref, dsk_ref, nw_ref, w_ref, o_ref):
    y = yf_ref[0, 0] + yb_ref[0, 0] + xs_ref[0] * dsk_ref[...]
    y = y * _silu(z_ref[0])
    gw = SSM_INNER // SSM_GROUPS
    parts = []
    for gi in range(SSM_GROUPS):
        yg = y[:, gi * gw:(gi + 1) * gw]
        ms = jnp.mean(yg * yg, axis=-1, keepdims=True)
        parts.append(yg * lax.rsqrt(ms + EPS) * nw_ref[:, gi * gw:(gi + 1) * gw])
    yn = jnp.concatenate(parts, axis=-1)
    o_ref[0] = x_ref[0] + g_ref[0] * _dot(yn.astype(BF16), w_ref[...])


def _ssd_finish(y2, xbc, z, x, g, dsk, nw, w, tm):
    b, l, d = x.shape
    return pl.pallas_call(
        _ssd_finish_kernel,
        grid=(b, l // tm),
        in_specs=[pl.BlockSpec((1, 1, tm, SSM_INNER), lambda b_, i: (0, b_, i, 0)),
                  pl.BlockSpec((1, 1, tm, SSM_INNER), lambda b_, i: (1, b_, i, 0)),
                  pl.BlockSpec((1, tm, SSM_INNER), lambda b_, i: (b_, i, 0)),
                  pl.BlockSpec((1, tm, SSM_INNER), lambda b_, i: (b_, i, 0)),
                  pl.BlockSpec((1, tm, d), lambda b_, i: (b_, i, 0)),
                  pl.BlockSpec((1, 1, d), _mod_index(g.shape[0])),
                  pl.BlockSpec((1, SSM_INNER), lambda b_, i: (0, 0)),
                  pl.BlockSpec((1, SSM_INNER), lambda b_, i: (0, 0)),
                  pl.BlockSpec((SSM_INNER, d), lambda b_, i: (0, 0))],
        out_specs=pl.BlockSpec((1, tm, d), lambda b_, i: (b_, i, 0)),
        out_shape=jax.ShapeDtypeStruct((b, l, d), F32),
        compiler_params=_params(2),
        name="ssd_finish",
    )(y2, y2, xbc, z, x, g, dsk, nw, w)


def _ssd_weights(w_in, conv_w, conv_b, dt_bias, a_log, d_skip, norm_w, w_out):
    d = w_in.shape[0]
    cc = SSD_COL_CHUNK
    n = SSM_XBC // cc
    wz = w_in[:, :SSM_INNER].astype(BF16)
    wx = jnp.transpose(w_in[:, SSM_INNER:SSM_INNER + SSM_XBC].reshape(d, n, cc), (1, 0, 2)).astype(BF16)
    wdt_raw = w_in[:, SSM_INNER + SSM_XBC:]
    pad = jnp.zeros((d, LANES - SSM_HEADS), F32)
    wdt = jnp.concatenate([wdt_raw[:, :SSM_HEADS], pad, wdt_raw[:, SSM_HEADS:], pad], axis=1).astype(BF16)
    bpad = jnp.zeros((LANES - SSM_HEADS,), F32)
    dtb = jnp.concatenate([dt_bias[0], bpad, dt_bias[1], bpad])[None]
    cw = jnp.transpose(conv_w.reshape(3, n, cc), (1, 0, 2))
    cb = conv_b.reshape(n, 1, cc)
    a = -jnp.exp(a_log.astype(F32))
    a_rows = jnp.concatenate([a, jnp.zeros((2, LANES - SSM_HEADS), F32)], axis=1).reshape(2, 1, LANES)
    dsk = jnp.repeat(d_skip, SSM_HEADDIM)[None]
    return wz, wx, wdt, cw, cb, dtb, a_rows, dsk, norm_w[None], w_out.astype(BF16)


def _ssd_mixer(xc, xl, mod_c, mod_l, nw, wts, tables, ctx_out, tm_l, tm_c):
    a_rows, dsk, norm_w, w_out = wts[6:]
    (csc, csh, cg), (sc, sh, g) = mod_c, mod_l
    zc, xbc_c, dtc = _ssd_project(xc, csc, csh, nw, wts, tm_c)
    zl, xbc_l, dtl = _ssd_project(xl, sc, sh, nw, wts, tm_l)
    zero = jnp.zeros((2, xc.shape[0], SSM_INNER, SSM_STATE), F32)
    yc, s_ctx = _ssd_scan(xbc_c, dtc, a_rows, tables, zero)
    yl, _ = _ssd_scan(xbc_l, dtl, a_rows, tables, s_ctx)
    xl = _ssd_finish(yl, xbc_l, zl, xl, g, dsk, norm_w, w_out, tm_l)
    if ctx_out:
        xc = _ssd_finish(yc, xbc_c, zc, xc, cg, dsk, norm_w, w_out, tm_c)
    return xc, xl


def kernel(x, c, ctx, c_ctx, ada_w, ada_b, norm_mix_w, norm_ffn_w, ffn_w_up, ffn_conv_w, ffn_conv_b, ffn_w_down, gla_w_in, gla_w_g1, gla_w_g2, gla_b_g, gla_norm_w, gla_w_out, mla_w_in, mla_q_a_norm, mla_w_uq, mla_kv_a_norm, mla_w_ukv, mla_q_norm, mla_k_norm, mla_w_out, ssm_w_in, ssm_conv_w, ssm_conv_b, ssm_dt_bias, ssm_a_log, ssm_d, ssm_norm_w, ssm_w_out):
    bsz, l, d = x.shape
    lc = ctx.shape[1]
    tm_l = min(512, l)
    tm_c = lc
    cond = jnp.concatenate([c, c_ctx[None], jnp.zeros((8 - bsz - 1, d), F32)], axis=0)
    mods = _modulation(cond, ada_w, ada_b)
    gla_tables = _gla_tables(SCAN_CHUNK)
    ssd_tables = _ssd_tables(SCAN_CHUNK)
    xl, xc = x, ctx
    for i in range(DEPTH):
        kind, j = i % N_MIXERS, i // N_MIXERS
        ctx_out = i < DEPTH - 1
        m6 = [mods[i, :, k * d:(k + 1) * d] for k in range(N_MOD)]
        lat = [m[:bsz, None, :] for m in m6]
        cx = [m[bsz:bsz + 1, None, :] for m in m6]
        nw = norm_mix_w[i][None]
        mod_l, mod_c = (lat[1], lat[0], lat[2]), (cx[1], cx[0], cx[2])
        if kind == 0:
            wts = _gla_weights(gla_w_in[j], gla_w_g1[j], gla_w_g2[j], gla_b_g[j], gla_norm_w[j], gla_w_out[j])
            xc, xl = _gla_mixer(xc, xl, mod_c, mod_l, nw, wts, gla_tables, ctx_out, tm_l, tm_c)
        elif kind == 1:
            wts = _mla_weights(mla_w_in[j], mla_q_a_norm[j], mla_w_uq[j], mla_kv_a_norm[j], mla_w_ukv[j],
                               mla_q_norm[j], mla_k_norm[j], mla_w_out[j])
            xc, xl = _mla_mixer(xc, xl, mod_c, mod_l, nw, wts, ctx_out, tm_l, tm_c)
        else:
            wts = _ssd_weights(ssm_w_in[j], ssm_conv_w[j], ssm_conv_b[j], ssm_dt_bias[j], ssm_a_log[j], ssm_d[j],
                               ssm_norm_w[j], ssm_w_out[j])
            xc, xl = _ssd_mixer(xc, xl, mod_c, mod_l, nw, wts, ssd_tables, ctx_out, tm_l, tm_c)
        fw = _ffn_weights(ffn_w_up[i], ffn_conv_w[i], ffn_conv_b[i], ffn_w_down[i])
        nfw = norm_ffn_w[i][None]
        xl = _conv_ffn_residual(xl, lat[4], lat[3], lat[5], nfw, fw, tm_l)
        if ctx_out:
            xc = _conv_ffn_residual(xc, cx[4], cx[3], cx[5], nfw, fw, tm_c)
    return xl
```

```python
import functools

import numpy as np
import jax
import jax.numpy as jnp
from jax import lax
from jax.experimental import pallas as pl
from jax.experimental.pallas import tpu as pltpu

F32 = jnp.float32
BF16 = jnp.bfloat16

D_MODEL = 1024
DEPTH = 4
GRID_W = 64
N_MIXERS = 3
N_MOD = 6
EPS = 1e-6

GLA_HEADS = 4
GLA_DK = D_MODEL // 2
GLA_DV = D_MODEL
GLA_HK = GLA_DK // GLA_HEADS
GLA_HV = GLA_DV // GLA_HEADS
GLA_GATE_RANK = 16
GLA_TAU = 16.0
GLA_IN = 2 * GLA_DK + 2 * GLA_DV

MLA_HEADS = 8
MLA_NOPE = 128
MLA_ROPE = 64
MLA_QK = MLA_NOPE + MLA_ROPE
MLA_V = 128
MLA_Q_RANK = 256
MLA_KV_RANK = 128
ROPE_THETA = 10000.0

SSM_INNER = 2 * D_MODEL
SSM_HEADDIM = 64
SSM_HEADS = SSM_INNER // SSM_HEADDIM
SSM_STATE = 128
SSM_GROUPS = 4
SSM_BC = SSM_GROUPS * SSM_STATE
SSM_XBC = SSM_INNER + 2 * SSM_BC

FFN_HIDDEN = 2816

LANES = 128
HALO = 8
VMEM_LIMIT_BYTES = 56 * 1024 * 1024
SCAN_CHUNK = 128
FFN_COL_CHUNK = 256
SSD_COL_CHUNK = 512


def _params(n_axes):
    return pltpu.CompilerParams(dimension_semantics=("arbitrary",) * n_axes,
                                vmem_limit_bytes=VMEM_LIMIT_BYTES)


def _dot(a, b):
    return jnp.dot(a, b, preferred_element_type=F32)


def _dot_nt(a, b):
    return lax.dot_general(a, b, (((1,), (1,)), ((), ())), preferred_element_type=F32)


def _dot_tn(a, b):
    return lax.dot_general(a, b, (((0,), (0,)), ((), ())), preferred_element_type=F32)


def _split_dot(mat_bf16, v_f32):
    hi = v_f32.astype(BF16)
    lo = (v_f32 - hi.astype(F32)).astype(BF16)
    return _dot(mat_bf16, hi) + _dot(mat_bf16, lo)


def _norm_mod(x, nw, sc, sh):
    ms = jnp.mean(x * x, axis=-1, keepdims=True)
    return (x * lax.rsqrt(ms + EPS) * nw) * (1.0 + sc) + sh


def _silu(x):
    return x * jax.nn.sigmoid(x)


def _mod_index(n_mod):
    if n_mod == 1:
        return lambda b, i: (0, 0, 0)
    return lambda b, i: (b, 0, 0)


def _modulation_kernel(cond_ref, w_ref, b_ref, o_ref):
    a = _silu(cond_ref[...]).astype(BF16)
    o_ref[0] = _dot(a, w_ref[0].astype(BF16)) + b_ref[0]


def _modulation(cond, ada_w, ada_b):
    r, d = cond.shape
    depth, _, n = ada_w.shape
    tn = 512
    return pl.pallas_call(
        _modulation_kernel,
        grid=(depth, n // tn),
        in_specs=[pl.BlockSpec((r, d), lambda l, j: (0, 0)),
                  pl.BlockSpec((1, d, tn), lambda l, j: (l, 0, j)),
                  pl.BlockSpec((1, 1, tn), lambda l, j: (l, 0, j))],
        out_specs=pl.BlockSpec((1, r, tn), lambda l, j: (l, 0, j)),
        out_shape=jax.ShapeDtypeStruct((depth, r, n), F32),
        compiler_params=_params(2),
        name="adaln_modulation",
    )(cond, ada_w, ada_b.reshape(depth, 1, n))


def _halo_specs(tm, l, d):
    per = tm // HALO
    last = l // HALO - 1
    return [pl.BlockSpec((1, tm, d), lambda b, i: (b, i, 0)),
            pl.BlockSpec((1, HALO, d), lambda b, i: (b, jnp.maximum(i * per - 1, 0), 0)),
            pl.BlockSpec((1, HALO, d), lambda b, i: (b, jnp.minimum((i + 1) * per, last), 0))]


def _halo_rows(x_ref, xp_ref, xn_ref, nw, sc, sh):
    i = pl.program_id(1)
    has_prev = (i > 0).astype(F32)
    has_next = (i < pl.num_programs(1) - 1).astype(F32)
    hp = _norm_mod(xp_ref[0], nw, sc, sh) * has_prev
    hm = _norm_mod(x_ref[0], nw, sc, sh)
    hn = _norm_mod(xn_ref[0], nw, sc, sh) * has_next
    return jnp.concatenate([hp, hm, hn], axis=0).astype(BF16)


def _conv3(u_ref, tm, w, b):
    return (u_ref[pl.ds(HALO - 1, tm), :] * w[0:1] + u_ref[pl.ds(HALO, tm), :] * w[1:2]
            + u_ref[pl.ds(HALO + 1, tm), :] * w[2:3] + b)


def _ffn_kernel(x_ref, xp_ref, xn_ref, sc_ref, sh_ref, g_ref, nw_ref, wg_ref, wv_ref, cwg_ref, cwv_ref,
                cbg_ref, cbv_ref, wd_ref, o_ref, h_buf, ug_buf, uv_buf, *, tm, n_chunks):
    h_buf[...] = _halo_rows(x_ref, xp_ref, xn_ref, nw_ref[...], sc_ref[0], sh_ref[0])
    acc = jnp.zeros((tm, D_MODEL), F32)
    for c in range(n_chunks):
        ug_buf[...] = _dot(h_buf[...], wg_ref[c])
        uv_buf[...] = _dot(h_buf[...], wv_ref[c])
        gate = _conv3(ug_buf, tm, cwg_ref[c], cbg_ref[c])
        val = _conv3(uv_buf, tm, cwv_ref[c], cbv_ref[c])
        acc = acc + _dot((_silu(gate) * val).astype(BF16), wd_ref[c])
    o_ref[0] = x_ref[0] + g_ref[0] * acc


def _conv_ffn_residual(x, sc, sh, g, nw, wts, tm):
    b, l, d = x.shape
    wg, wv, cwg, cwv, cbg, cbv, wd = wts
    n_chunks, _, fc = wg.shape
    mod = pl.BlockSpec((1, 1, d), _mod_index(sc.shape[0]))
    full3 = lambda a: pl.BlockSpec(a.shape, lambda b_, i: (0, 0, 0))
    kern = functools.partial(_ffn_kernel, tm=tm, n_chunks=n_chunks)
    return pl.pallas_call(
        kern,
        grid=(b, l // tm),
        in_specs=_halo_specs(tm, l, d) + [mod, mod, mod, pl.BlockSpec((1, d), lambda b_, i: (0, 0)),
                                           full3(wg), full3(wv), full3(cwg), full3(cwv), full3(cbg), full3(cbv),
                                           full3(wd)],
        out_specs=pl.BlockSpec((1, tm, d), lambda b_, i: (b_, i, 0)),
        out_shape=jax.ShapeDtypeStruct((b, l, d), F32),
        scratch_shapes=[pltpu.VMEM((tm + 2 * HALO, d), BF16),
                        pltpu.VMEM((tm + 2 * HALO, fc), F32),
                        pltpu.VMEM((tm + 2 * HALO, fc), F32)],
        compiler_params=_params(2),
        name="conv_ffn",
    )(x, x, x, sc, sh, g, nw, wg, wv, cwg, cwv, cbg, cbv, wd)


def _ffn_weights(w_up, conv_w, conv_b, w_down):
    f, fc = FFN_HIDDEN, FFN_COL_CHUNK
    n = f // fc
    cols = lambda w: jnp.transpose(w.reshape(w.shape[0], n, fc), (1, 0, 2))
    return (cols(w_up[:, :f]).astype(BF16), cols(w_up[:, f:]).astype(BF16),
            cols(conv_w[:, :f]), cols(conv_w[:, f:]),
            cols(conv_b[None, :f]), cols(conv_b[None, f:]),
            w_down.reshape(n, fc, D_MODEL).astype(BF16))


def _out_proj_kernel(a_ref, x_ref, g_ref, w_ref, o_ref):
    o_ref[0] = x_ref[0] + g_ref[0] * _dot(a_ref[0].astype(BF16), w_ref[...])


def _out_proj_residual(a, x, g, w, tm):
    b, l, d = x.shape
    k = a.shape[-1]
    return pl.pallas_call(
        _out_proj_kernel,
        grid=(b, l // tm),
        in_specs=[pl.BlockSpec((1, tm, k), lambda b_, i: (b_, i, 0)),
                  pl.BlockSpec((1, tm, d), lambda b_, i: (b_, i, 0)),
                  pl.BlockSpec((1, 1, d), _mod_index(g.shape[0])),
                  pl.BlockSpec((k, d), lambda b_, i: (0, 0))],
        out_specs=pl.BlockSpec((1, tm, d), lambda b_, i: (b_, i, 0)),
        out_shape=jax.ShapeDtypeStruct((b, l, d), F32),
        compiler_params=_params(2),
        name="out_proj_residual",
    )(a, x, g, w)


GLA_W1_COLS = GLA_IN + LANES


def _gla_proj_kernel(x_ref, sc_ref, sh_ref, nw_ref, w1_ref, w2_ref, bg_ref, p_ref, lg_ref):
    h = _norm_mod(x_ref[0], nw_ref[...], sc_ref[0], sh_ref[0]).astype(BF16)
    p = _dot(h, w1_ref[...])
    p_ref[0] = p[:, :GLA_IN]
    z = _dot(p[:, GLA_IN:].astype(BF16), w2_ref[...]) + bg_ref[...]
    lg_ref[0] = jax.nn.log_sigmoid(z) * (1.0 / GLA_TAU)


def _gla_project(x, sc, sh, nw, w1, w2, bg, tm):
    b, l, d = x.shape
    mod = pl.BlockSpec((1, 1, d), _mod_index(sc.shape[0]))
    const = lambda a: pl.BlockSpec(a.shape, lambda b_, i: (0, 0))
    return pl.pallas_call(
        _gla_proj_kernel,
        grid=(b, l // tm),
        in_specs=[pl.BlockSpec((1, tm, d), lambda b_, i: (b_, i, 0)), mod, mod, const(nw), const(w1), const(w2),
                  const(bg)],
        out_specs=[pl.BlockSpec((1, tm, GLA_IN), lambda b_, i: (b_, i, 0)),
                   pl.BlockSpec((1, tm, 2 * GLA_DK), lambda b_, i: (b_, i, 0))],
        out_shape=[jax.ShapeDtypeStruct((b, l, GLA_IN), F32), jax.ShapeDtypeStruct((b, l, 2 * GLA_DK), F32)],
        compiler_params=_params(2),
        name="gla_in_proj",
    )(x, sc, sh, nw, w1, w2, bg)


def _gla_tables(c):
    n_lev = int(np.log2(c))
    idx = np.arange(c)
    i, t = idx[:, None], idx[None, :]
    dec = np.zeros(((n_lev + 2) * c + 8, c), np.float32)
    dec[0:c] = t <= i
    dec[c:2 * c] = t > i
    mask = np.zeros((n_lev + 1, c, c), np.float32)
    mask[0] = np.eye(c)
    for lev in range(n_lev):
        m = c >> (lev + 1)
        mid = (i // (2 * m)) * 2 * m + m
        upper = (i % (2 * m)) >= m
        dec[(2 + lev) * c:(3 + lev) * c] = np.where(upper, (t >= mid) & (t <= i), (t > i) & (t < mid))
        j = t
        mask[lev + 1] = (i // (2 * m) == j // (2 * m)) & upper & ((j % (2 * m)) < m)
    dec[(n_lev + 2) * c:] = 1.0
    rev = lambda a: a[..., ::-1, ::-1]
    dec_b = np.concatenate([np.concatenate([rev(dec[k * c:(k + 1) * c]) for k in range(n_lev + 2)], 0),
                            dec[(n_lev + 2) * c:]], 0)
    return (jnp.asarray(np.stack([dec, dec_b]), BF16), jnp.asarray(np.stack([mask, rev(mask)]), F32))


def _gla_scan_kernel(q_ref, k_ref, v_ref, lg_ref, dec_ref, mask_ref, s0_ref, o_ref, sT_ref, state, *, c, n_lev):
    @pl.when(pl.program_id(2) == 0)
    def _():
        state[...] = s0_ref[0, 0]

    for h in range(GLA_HEADS):
        ks = slice(h * GLA_HK, (h + 1) * GLA_HK)
        vs = slice(h * GLA_HV, (h + 1) * GLA_HV)
        q = q_ref[0, :, ks] * (GLA_HK ** -0.5)
        k = k_ref[0, :, ks]
        vb = v_ref[0, :, vs].astype(BF16)
        lg = lg_ref[0, :, ks]
        hi = lg.astype(BF16)
        lo = (lg - hi.astype(F32)).astype(BF16)
        d2 = _dot(dec_ref[0], jnp.concatenate([hi, lo], axis=1))
        e_all = jnp.exp(d2[:, :GLA_HK] + d2[:, GLA_HK:])
        att = mask_ref[0, 0] * _dot_nt(q.astype(BF16), k.astype(BF16))
        for lev in range(n_lev):
            e = e_all[(2 + lev) * c:(3 + lev) * c]
            att = att + mask_ref[0, lev + 1] * _dot_nt((q * e).astype(BF16), (k * e).astype(BF16))
        st = state[h]
        o_ref[0, 0, :, vs] = _dot(att.astype(BF16), vb) + _dot_nt((q * e_all[0:c]).astype(BF16), st.astype(BF16))
        e_tot = e_all[(n_lev + 2) * c:(n_lev + 2) * c + 1]
        st = st * e_tot + _dot_tn(vb, (k * e_all[c:2 * c]).astype(BF16))
        state[h] = st
        sT_ref[0, 0, h] = st


def _gla_scan(proj, lg, tables, s0):
    b, l, _ = proj.shape
    c = SCAN_CHUNK
    n = l // c
    dec, mask = tables
    n_lev = mask.shape[1] - 1
    chunk = lambda d, i: i + d * (n - 1 - 2 * i)
    kern = functools.partial(_gla_scan_kernel, c=c, n_lev=n_lev)
    st_spec = pl.BlockSpec((1, 1, GLA_HEADS, GLA_HV, GLA_HK), lambda d, b_, i: (d, b_, 0, 0, 0))
    return pl.pallas_call(
        kern,
        grid=(2, b, n),
        in_specs=[pl.BlockSpec((1, c, GLA_DK), lambda d, b_, i: (b_, chunk(d, i), 0)),
                  pl.BlockSpec((1, c, GLA_DK), lambda d, b_, i: (b_, chunk(d, i), 1)),
                  pl.BlockSpec((1, c, GLA_DV), lambda d, b_, i: (b_, chunk(d, i), 1)),
                  pl.BlockSpec((1, c, GLA_DK), lambda d, b_, i: (b_, chunk(d, i), d)),
                  pl.BlockSpec((1,) + dec.shape[1:], lambda d, b_, i: (d, 0, 0)),
                  pl.BlockSpec((1,) + mask.shape[1:], lambda d, b_, i: (d, 0, 0, 0)),
                  st_spec],
        out_specs=[pl.BlockSpec((1, 1, c, GLA_DV), lambda d, b_, i: (d, b_, chunk(d, i), 0)), st_spec],
        out_shape=[jax.ShapeDtypeStruct((2, b, l, GLA_DV), F32),
                   jax.ShapeDtypeStruct((2, b, GLA_HEADS, GLA_HV, GLA_HK), F32)],
        scratch_shapes=[pltpu.VMEM((GLA_HEADS, GLA_HV, GLA_HK), F32)],
        compiler_params=_params(3),
        name="gla_scan",
    )(proj, proj, proj, lg, dec, mask, s0)


def _gla_finish_kernel(of_ref, ob_ref, r_ref, x_ref, g_ref, nw_ref, w_ref, o_ref):
    o = of_ref[0, 0] + ob_ref[0, 0]
    heads = []
    for h in range(GLA_HEADS):
        oh = o[:, h * GLA_HV:(h + 1) * GLA_HV]
        ms = jnp.mean(oh * oh, axis=-1, keepdims=True)
        heads.append(oh * lax.rsqrt(ms + EPS) * nw_ref[...])
    y = jnp.concatenate(heads, axis=-1) * _silu(r_ref[0])
    o_ref[0] = x_ref[0] + g_ref[0] * _dot(y.astype(BF16), w_ref[...])


def _gla_finish(o2, proj, x, g, nw, w, tm):
    b, l, d = x.shape
    rb = (2 * GLA_DK + GLA_DV) // GLA_DV
    return pl.pallas_call(
        _gla_finish_kernel,
        grid=(b, l // tm),
        in_specs=[pl.BlockSpec((1, 1, tm, GLA_DV), lambda b_, i: (0, b_, i, 0)),
                  pl.BlockSpec((1, 1, tm, GLA_DV), lambda b_, i: (1, b_, i, 0)),
                  pl.BlockSpec((1, tm, GLA_DV), lambda b_, i: (b_, i, rb)),
                  pl.BlockSpec((1, tm, d), lambda b_, i: (b_, i, 0)),
                  pl.BlockSpec((1, 1, d), _mod_index(g.shape[0])),
                  pl.BlockSpec((1, GLA_HV), lambda b_, i: (0, 0)),
                  pl.BlockSpec((GLA_DV, d), lambda b_, i: (0, 0))],
        out_specs=pl.BlockSpec((1, tm, d), lambda b_, i: (b_, i, 0)),
        out_shape=jax.ShapeDtypeStruct((b, l, d), F32),
        compiler_params=_params(2),
        name="gla_finish",
    )(o2, o2, proj, x, g, nw, w)


def _gla_weights(w_in, w_g1, w_g2, b_g, norm_w, w_out):
    d = w_in.shape[0]
    r = GLA_GATE_RANK
    w1 = jnp.concatenate([w_in, w_g1[0], w_g1[1], jnp.zeros((d, LANES - 2 * r), F32)], axis=1).astype(BF16)
    w2 = jnp.zeros((LANES, 2 * GLA_DK), F32)
    w2 = w2.at[0:r, :GLA_DK].set(w_g2[0]).at[r:2 * r, GLA_DK:].set(w_g2[1]).astype(BF16)
    return w1, w2, b_g.reshape(1, 2 * GLA_DK), norm_w.reshape(1, GLA_HV), w_out.astype(BF16)


def _gla_mixer(xc, xl, mod_c, mod_l, nw, wts, tables, ctx_out, tm_l, tm_c):
    w1, w2, bg, norm_w, w_out = wts
    (csc, csh, cg), (sc, sh, g) = mod_c, mod_l
    pc, lgc = _gla_project(xc, csc, csh, nw, w1, w2, bg, tm_c)
    plat, lgl = _gla_project(xl, sc, sh, nw, w1, w2, bg, tm_l)
    zero = jnp.zeros((2, xc.shape[0], GLA_HEADS, GLA_HV, GLA_HK), F32)
    oc, s_ctx = _gla_scan(pc, lgc, tables, zero)
    ol, _ = _gla_scan(plat, lgl, tables, s_ctx)
    xl = _gla_finish(ol, plat, xl, g, norm_w, w_out, tm_l)
    if ctx_out:
        xc = _gla_finish(oc, pc, xc, cg, norm_w, w_out, tm_c)
    return xc, xl


MLA_HEAD_PAD = 2 * LANES
MLA_V_PAD = 2 * LANES


def _rot_cols(w):
    half = MLA_ROPE // 2
    return jnp.concatenate([-w[..., half:], w[..., :half]], axis=-1)


def _swap_halves(w):
    half = MLA_ROPE // 2
    return jnp.concatenate([w[..., half:], w[..., :half]], axis=-1)


def _mla_proj_kernel(x_ref, sc_ref, sh_ref, nw_ref, win_ref, qan_ref, kvan_ref, wuq_ref, wukv_ref,
                     qna_ref, qnb_ref, kna_ref, knb_ref, cs_ref, q_ref, k_ref, v_ref):
    h = _norm_mod(x_ref[0], nw_ref[...], sc_ref[0], sh_ref[0]).astype(BF16)
    p = _dot(h, win_ref[...])
    cq = p[:, :MLA_Q_RANK]
    ckv = p[:, MLA_Q_RANK:MLA_Q_RANK + MLA_KV_RANK]
    kpe = p[:, MLA_Q_RANK + MLA_KV_RANK:]
    cq = cq * lax.rsqrt(jnp.mean(cq * cq, axis=-1, keepdims=True) + EPS) * qan_ref[...]
    ckv = ckv * lax.rsqrt(jnp.mean(ckv * ckv, axis=-1, keepdims=True) + EPS) * kvan_ref[...]
    qall = _dot(cq.astype(BF16), wuq_ref[...])
    kvall = _dot(ckv.astype(BF16), wukv_ref[...])
    cs = cs_ref[...]
    lane = lax.broadcasted_iota(jnp.int32, (1, LANES), 1)
    keep = (lane < MLA_ROPE).astype(F32)
    kpe_ss = 0.5 * jnp.sum(kpe * kpe, axis=-1, keepdims=True)
    scale = MLA_QK ** -0.5 * np.log2(np.e)

    def head(nope, pe, na, nb, mult):
        ss = jnp.sum(nope * nope, axis=-1, keepdims=True) + 0.5 * jnp.sum(pe * pe, axis=-1, keepdims=True)
        s = lax.rsqrt(ss * (1.0 / MLA_QK) + EPS)
        t = pe * s * nb * cs
        rope = (t + pltpu.roll(t, MLA_ROPE, 1)) * keep
        return jnp.concatenate([nope * s * na * mult, rope * mult], axis=-1).astype(BF16)

    for hd in range(MLA_HEADS):
        qh = qall[:, hd * MLA_HEAD_PAD:(hd + 1) * MLA_HEAD_PAD]
        q_ref[0, hd] = head(qh[:, :LANES], qh[:, LANES:], qna_ref[...], qnb_ref[...], scale)
        kh = kvall[:, hd * 2 * LANES:hd * 2 * LANES + LANES]
        ss = jnp.sum(kh * kh, axis=-1, keepdims=True) + kpe_ss
        s = lax.rsqrt(ss * (1.0 / MLA_QK) + EPS)
        t = kpe * s * knb_ref[...] * cs
        rope = (t + pltpu.roll(t, MLA_ROPE, 1)) * keep
        k_ref[0, hd] = jnp.concatenate([kh * s * kna_ref[...], rope], axis=-1).astype(BF16)
        vh = kvall[:, hd * 2 * LANES + LANES:(hd + 1) * 2 * LANES]
        v_ref[0, hd] = jnp.concatenate([vh, jnp.ones_like(vh)], axis=-1).astype(BF16)


def _mla_project(x, sc, sh, nw, wts, cs, tm):
    b, l, d = x.shape
    win, qan, kvan, wuq, wukv, qna, qnb, kna, knb = wts[:9]
    mod = pl.BlockSpec((1, 1, d), _mod_index(sc.shape[0]))
    const = lambda a: pl.BlockSpec(a.shape, lambda b_, i: (0, 0))
    cs_spec = (pl.BlockSpec((tm, LANES), lambda b_, i: (i, 0)) if cs.shape[0] == l
               else pl.BlockSpec((tm, LANES), lambda b_, i: (0, 0)))
    hspec = lambda w: pl.BlockSpec((1, MLA_HEADS, tm, w), lambda b_, i: (b_, 0, i, 0))
    return pl.pallas_call(
        _mla_proj_kernel,
        grid=(b, l // tm),
        in_specs=[pl.BlockSpec((1, tm, d), lambda b_, i: (b_, i, 0)), mod, mod, const(nw), const(win), const(qan),
                  const(kvan), const(wuq), const(wukv), const(qna), const(qnb), const(kna), const(knb), cs_spec],
        out_specs=[hspec(MLA_HEAD_PAD), hspec(MLA_HEAD_PAD), hspec(MLA_V_PAD)],
        out_shape=[jax.ShapeDtypeStruct((b, MLA_HEADS, l, MLA_HEAD_PAD), BF16),
                   jax.ShapeDtypeStruct((b, MLA_HEADS, l, MLA_HEAD_PAD), BF16),
                   jax.ShapeDtypeStruct((b, MLA_HEADS, l, MLA_V_PAD), BF16)],
        compiler_params=_params(2),
        name="mla_project",
    )(x, sc, sh, nw, win, qan, kvan, wuq, wukv, qna, qnb, kna, knb, cs)


def _flash_kernel(*refs, tk, n_kv, has_ctx):
    if has_ctx:
        q_ref, k_ref, v_ref, kc_ref, vc_ref, o_ref = refs
    else:
        q_ref, k_ref, v_ref, o_ref = refs
    q = q_ref[0, 0]

    def update(carry, kj, vj):
        m, acc = carry
        s = _dot_nt(q, kj)
        m_new = jnp.maximum(m, jnp.max(s, axis=-1, keepdims=True))
        p = jnp.exp2(s - m_new)
        return m_new, jnp.exp2(m - m_new) * acc + _dot(p.astype(BF16), vj)

    tq = q.shape[0]
    carry = (jnp.full((tq, 1), -jnp.inf, F32), jnp.zeros((tq, MLA_V_PAD), F32))
    if has_ctx:
        carry = update(carry, kc_ref[0, 0], vc_ref[0, 0])
    for j in range(n_kv):
        carry = update(carry, k_ref[0, 0, j * tk:(j + 1) * tk, :], v_ref[0, 0, j * tk:(j + 1) * tk, :])
    acc = carry[1]
    o_ref[0] = (acc[:, :MLA_V] / acc[:, MLA_V:]).astype(o_ref.dtype)


def _flash_attention(q, k, v, kc, vc, tq, tk):
    b, hds, l, dq = q.shape
    lk = k.shape[2]
    has_ctx = kc is not None
    kern = functools.partial(_flash_kernel, tk=tk, n_kv=lk // tk, has_ctx=has_ctx)
    whole = lambda a: pl.BlockSpec((1, 1) + a.shape[2:], lambda b_, h, i: (b_, h, 0, 0))
    in_specs = [pl.BlockSpec((1, 1, tq, dq), lambda b_, h, i: (b_, h, i, 0)), whole(k), whole(v)]
    args = [q, k, v]
    if has_ctx:
        in_specs += [whole(kc), whole(vc)]
        args += [kc, vc]
    return pl.pallas_call(
        kern,
        grid=(b, hds, l // tq),
        in_specs=in_specs,
        out_specs=pl.BlockSpec((1, tq, MLA_V), lambda b_, h, i: (b_, i, h)),
        out_shape=jax.ShapeDtypeStruct((b, l, hds * MLA_V), BF16),
        compiler_params=_params(3),
        name="mla_flash_attention",
    )(*args)


def _mla_weights(w_in, q_a_norm, w_uq, kv_a_norm, w_ukv, q_norm, k_norm, w_out):
    kpe = w_in[:, MLA_Q_RANK + MLA_KV_RANK:]
    win = jnp.concatenate([w_in, _rot_cols(kpe)], axis=1).astype(BF16)
    wq = w_uq.reshape(MLA_Q_RANK, MLA_HEADS, MLA_QK)
    wq = jnp.concatenate([wq, _rot_cols(wq[..., MLA_NOPE:])], axis=-1)
    wuq = wq.reshape(MLA_Q_RANK, MLA_HEADS * MLA_HEAD_PAD).astype(BF16)
    gains = lambda nrm: (nrm[None, :MLA_NOPE],
                         jnp.concatenate([nrm[MLA_NOPE:], _swap_halves(nrm[MLA_NOPE:])])[None])
    qna, qnb = gains(q_norm)
    kna, knb = gains(k_norm)
    return (win, q_a_norm[None], kv_a_norm[None], wuq, w_ukv.astype(BF16), qna, qnb, kna, knb, w_out.astype(BF16))


def _rope_table(l):
    rows = l // GRID_W
    pos_r = jnp.repeat(jnp.arange(rows, dtype=F32), GRID_W)
    pos_c = jnp.tile(jnp.arange(GRID_W, dtype=F32), rows)
    n_freq = MLA_ROPE // 4
    inv = ROPE_THETA ** (-jnp.arange(n_freq, dtype=F32) / n_freq)
    ang = jnp.concatenate([pos_r[:, None] * inv, pos_c[:, None] * inv], axis=-1)
    return jnp.concatenate([jnp.cos(ang), jnp.cos(ang), jnp.sin(ang), jnp.sin(ang)], axis=-1)


def _mla_mixer(xc, xl, mod_c, mod_l, nw, wts, ctx_out, tm_l, tm_c):
    (csc, csh, cg), (sc, sh, g) = mod_c, mod_l
    w_out = wts[9]
    lc, l = xc.shape[1], xl.shape[1]
    no_rope = jnp.concatenate([jnp.ones((tm_c, LANES // 2), F32), jnp.zeros((tm_c, LANES // 2), F32)], axis=-1)
    qc, kc, vc = _mla_project(xc, csc, csh, nw, wts, no_rope, tm_c)
    ql, kl, vl = _mla_project(xl, sc, sh, nw, wts, _rope_table(l), tm_l)
    tq = min(512, l)
    al = _flash_attention(ql, kl, vl, kc, vc, tq, min(512, l))
    xl = _out_proj_residual(al, xl, g, w_out, tm_l)
    if ctx_out:
        ac = _flash_attention(qc, kc, vc, None, None, lc, lc)
        xc = _out_proj_residual(ac, xc, cg, w_out, tm_c)
    return xc, xl


def _ssd_proj_kernel(x_ref, xp_ref, xn_ref, sc_ref, sh_ref, nw_ref, wz_ref, wx_ref, wdt_ref, cw_ref, cb_ref,
                     dtb_ref, z_ref, xbc_ref, dt_ref, h_buf, u_buf, *, tm, n_chunks, cc):
    h_buf[...] = _halo_rows(x_ref, xp_ref, xn_ref, nw_ref[...], sc_ref[0], sh_ref[0])
    hm = h_buf[pl.ds(HALO, tm), :]
    z_ref[0] = _dot(hm, wz_ref[...])
    dt = jax.nn.softplus(_dot(hm, wdt_ref[...]) + dtb_ref[...])
    dt_ref[0, 0] = dt[:, :LANES]
    dt_ref[1, 0] = dt[:, LANES:]
    for c in range(n_chunks):
        u_buf[...] = _dot(h_buf[...], wx_ref[c])
        xbc_ref[0, :, c * cc:(c + 1) * cc] = _silu(_conv3(u_buf, tm, cw_ref[c], cb_ref[c]))


def _ssd_project(x, sc, sh, nw, wts, tm):
    b, l, d = x.shape
    wz, wx, wdt, cw, cb, dtb = wts[:6]
    n_chunks, _, cc = wx.shape
    mod = pl.BlockSpec((1, 1, d), _mod_index(sc.shape[0]))
    const2 = lambda a: pl.BlockSpec(a.shape, lambda b_, i: (0, 0))
    const3 = lambda a: pl.BlockSpec(a.shape, lambda b_, i: (0, 0, 0))
    kern = functools.partial(_ssd_proj_kernel, tm=tm, n_chunks=n_chunks, cc=cc)
    return pl.pallas_call(
        kern,
        grid=(b, l // tm),
        in_specs=_halo_specs(tm, l, d) + [mod, mod, const2(nw), const2(wz), const3(wx), const2(wdt), const3(cw),
                                           const3(cb), const2(dtb)],
        out_specs=[pl.BlockSpec((1, tm, SSM_INNER), lambda b_, i: (b_, i, 0)),
                   pl.BlockSpec((1, tm, SSM_XBC), lambda b_, i: (b_, i, 0)),
                   pl.BlockSpec((2, 1, tm, LANES), lambda b_, i: (0, b_, i, 0))],
        out_shape=[jax.ShapeDtypeStruct((b, l, SSM_INNER), F32),
                   jax.ShapeDtypeStruct((b, l, SSM_XBC), F32),
                   jax.ShapeDtypeStruct((2, b, l, LANES), F32)],
        scratch_shapes=[pltpu.VMEM((tm + 2 * HALO, d), BF16), pltpu.VMEM((tm + 2 * HALO, cc), F32)],
        compiler_params=_params(2),
        name="ssd_in_proj_conv",
    )(x, x, x, sc, sh, nw, wz, wx, wdt, cw, cb, dtb)


def _ssd_tables(c):
    idx = np.arange(c)
    i, t = idx[:, None], idx[None, :]
    ones = np.ones((8, c), np.float32)
    fwd = np.concatenate([(t <= i).astype(np.float32), ones], 0)
    bwd = np.concatenate([(t >= i).astype(np.float32), ones], 0)
    mask = np.stack([(t <= i), (t >= i)]).astype(np.float32)
    h = np.arange(SSM_HEADS)
    e_seg = np.zeros((LANES, SSM_HEADS, c), np.float32)
    e_seg[h, h, :] = 1.0
    e_seg[SSM_HEADS + h, h, :] = 1.0
    e_x = np.zeros((LANES, 2, SSM_HEADS, SSM_HEADDIM), np.float32)
    for q in range(2):
        e_x[2 * q * SSM_HEADS + h, q, h, :] = 1.0
        e_x[(2 * q + 1) * SSM_HEADS + h, q, h, :] = 1.0
    return (jnp.asarray(np.stack([fwd, bwd]), BF16), jnp.asarray(mask, F32),
            jnp.asarray(e_seg.reshape(LANES, SSM_HEADS * c), BF16),
            jnp.asarray(e_x.reshape(LANES, 2 * SSM_INNER), BF16))


def _pack_hi_lo(parts):
    out = None
    for n, v in enumerate(parts):
        hi = v.astype(BF16).astype(F32)
        for j, piece in enumerate((hi, v - hi)):
            shift = (2 * n + j) * SSM_HEADS
            piece = piece if shift == 0 else pltpu.roll(piece, shift, 1)
            out = piece if out is None else out + piece
    return out.astype(BF16)


def _ssd_scan_kernel(x_ref, b_ref, c_ref, dt_ref, a_ref, tri_ref, mask_ref, eseg_ref, ex_ref, s0_ref, y_ref, sf_ref,
                     state, *, c):
    @pl.when(pl.program_id(2) == 0)
    def _():
        state[...] = s0_ref[0, 0]

    lane = lax.broadcasted_iota(jnp.int32, (1, LANES), 1)
    valid = (lane < SSM_HEADS).astype(F32)
    dt = dt_ref[0, 0] * valid
    cums = _split_dot(tri_ref[0], dt * a_ref[0])
    cum = cums[:c]
    total = cums[c:c + 1]
    e_in = jnp.exp(cum) * valid
    w_out = jnp.exp(total - cum) * dt
    e_tot = jnp.exp(total)
    cum_col = _dot(_pack_hi_lo([cum]), eseg_ref[...])
    xs = _dot(_pack_hi_lo([e_in, w_out]), ex_ref[...])
    cum_t = jnp.transpose(cum)
    dt_t = jnp.transpose(dt)
    causal = mask_ref[0] > 0.5
    left = (lane < SSM_HEADDIM).astype(F32)
    per_group = SSM_HEADS // SSM_GROUPS
    gw = per_group * SSM_HEADDIM

    for g in range(SSM_GROUPS):
        bm = b_ref[0, :, g * SSM_STATE:(g + 1) * SSM_STATE].astype(BF16)
        cm = c_ref[0, :, g * SSM_STATE:(g + 1) * SSM_STATE].astype(BF16)
        scores = _dot_nt(cm, bm)
        sg = state[g * gw:(g + 1) * gw, :]
        y_in = _dot_nt(cm, sg.astype(BF16))
        x_g = x_ref[0, :, g * gw:(g + 1) * gw]
        for pr in range(per_group // 2):
            h0 = g * per_group + 2 * pr
            col = g * gw + pr * LANES
            ms = []
            for h in (h0, h0 + 1):
                seg = jnp.exp(cum_col[:, h * c:(h + 1) * c] - cum_t[h:h + 1, :])
                ms.append((scores * jnp.where(causal, seg, 0.0) * dt_t[h:h + 1, :]).astype(BF16))
            xp = x_g[:, pr * LANES:(pr + 1) * LANES]
            xbd = jnp.concatenate([xp * left, xp * (1.0 - left)], axis=0).astype(BF16)
            y = _dot(jnp.concatenate(ms, axis=1), xbd)
            y_ref[0, 0, :, col:col + LANES] = y + xs[:, col:col + LANES] * y_in[:, pr * LANES:(pr + 1) * LANES]
        wx = (x_g * xs[:, SSM_INNER + g * gw:SSM_INNER + (g + 1) * gw]).astype(BF16)
        upd = _dot_tn(wx, bm)
        dec = jnp.concatenate(
            [jnp.broadcast_to(e_tot[:, g * per_group + j:g * per_group + j + 1], (SSM_HEADDIM, SSM_STATE))
             for j in range(per_group)], axis=0)
        state[g * gw:(g + 1) * gw, :] = sg * dec + upd
    sf_ref[0, 0] = state[...]


def _ssd_scan(xbc, dt2, a_rows, tables, s0):
    b, l, _ = xbc.shape
    c = SCAN_CHUNK
    n = l // c
    tri, mask, e_seg, e_x = tables
    chunk = lambda d, i: i + d * (n - 1 - 2 * i)
    xb = SSM_INNER // SSM_BC
    kern = functools.partial(_ssd_scan_kernel, c=c)
    return pl.pallas_call(
        kern,
        grid=(2, b, n),
        in_specs=[pl.BlockSpec((1, c, SSM_INNER), lambda d, b_, i: (b_, chunk(d, i), 0)),
                  pl.BlockSpec((1, c, SSM_BC), lambda d, b_, i: (b_, chunk(d, i), xb)),
                  pl.BlockSpec((1, c, SSM_BC), lambda d, b_, i: (b_, chunk(d, i), xb + 1)),
                  pl.BlockSpec((1, 1, c, LANES), lambda d, b_, i: (d, b_, chunk(d, i), 0)),
                  pl.BlockSpec((1, 1, LANES), lambda d, b_, i: (d, 0, 0)),
                  pl.BlockSpec((1, c + 8, c), lambda d, b_, i: (d, 0, 0)),
                  pl.BlockSpec((1, c, c), lambda d, b_, i: (d, 0, 0)),
                  pl.BlockSpec(e_seg.shape, lambda d, b_, i: (0, 0)),
                  pl.BlockSpec(e_x.shape, lambda d, b_, i: (0, 0)),
                  pl.BlockSpec((1, 1, SSM_INNER, SSM_STATE), lambda d, b_, i: (d, b_, 0, 0))],
        out_specs=[pl.BlockSpec((1, 1, c, SSM_INNER), lambda d, b_, i: (d, b_, chunk(d, i), 0)),
                   pl.BlockSpec((1, 1, SSM_INNER, SSM_STATE), lambda d, b_, i: (d, b_, 0, 0))],
        out_shape=[jax.ShapeDtypeStruct((2, b, l, SSM_INNER), F32),
                   jax.ShapeDtypeStruct((2, b, SSM_INNER, SSM_STATE), F32)],
        scratch_shapes=[pltpu.VMEM((SSM_INNER, SSM_STATE), F32)],
        compiler_params=_params(3),
        name="ssd_scan",
    )(xbc, xbc, xbc, dt2, a_rows, tri, mask, e_seg, e_x, s0)


def _ssd_finish_kernel(yf_ref, yb_ref, xs_ref, z_ref, x_ref, g_ref, dsk_ref, nw_ref, w_ref, o_ref):
    y = yf_ref[0, 0] + yb_ref[0, 0] + xs_ref[0] * dsk_ref[...]
    y = y * _silu(z_ref[0])
    gw = SSM_INNER // SSM_GROUPS
    parts = []
    for gi in range(SSM_GROUPS):
        yg = y[:, gi * gw:(gi + 1) * gw]
        ms = jnp.mean(yg * yg, axis=-1, keepdims=True)
        parts.append(yg * lax.rsqrt(ms + EPS) * nw_ref[:, gi * gw:(gi + 1) * gw])
    yn = jnp.concatenate(parts, axis=-1)
    o_ref[0] = x_ref[0] + g_ref[0] * _dot(yn.astype(BF16), w_ref[...])


def _ssd_finish(y2, xbc, z, x, g, dsk, nw, w, tm):
    b, l, d = x.shape
    return pl.pallas_call(
        _ssd_finish_kernel,
        grid=(b, l // tm),
        in_specs=[pl.BlockSpec((1, 1, tm, SSM_INNER), lambda b_, i: (0, b_, i, 0)),
                  pl.BlockSpec((1, 1, tm, SSM_INNER), lambda b_, i: (1, b_, i, 0)),
                  pl.BlockSpec((1, tm, SSM_INNER), lambda b_, i: (b_, i, 0)),
                  pl.BlockSpec((1, tm, SSM_INNER), lambda b_, i: (b_, i, 0)),
                  pl.BlockSpec((1, tm, d), lambda b_, i: (b_, i, 0)),
                  pl.BlockSpec((1, 1, d), _mod_index(g.shape[0])),
                  pl.BlockSpec((1, SSM_INNER), lambda b_, i: (0, 0)),
                  pl.BlockSpec((1, SSM_INNER), lambda b_, i: (0, 0)),
                  pl.BlockSpec((SSM_INNER, d), lambda b_, i: (0, 0))],
        out_specs=pl.BlockSpec((1, tm, d), lambda b_, i: (b_, i, 0)),
        out_shape=jax.ShapeDtypeStruct((b, l, d), F32),
        compiler_params=_params(2),
        name="ssd_finish",
    )(y2, y2, xbc, z, x, g, dsk, nw, w)


def _ssd_weights(w_in, conv_w, conv_b, dt_bias, a_log, d_skip, norm_w, w_out):
    d = w_in.shape[0]
    cc = SSD_COL_CHUNK
    n = SSM_XBC // cc
    wz = w_in[:, :SSM_INNER].astype(BF16)
    wx = jnp.transpose(w_in[:, SSM_INNER:SSM_INNER + SSM_XBC].reshape(d, n, cc), (1, 0, 2)).astype(BF16)
    wdt_raw = w_in[:, SSM_INNER + SSM_XBC:]
    pad = jnp.zeros((d, LANES - SSM_HEADS), F32)
    wdt = jnp.concatenate([wdt_raw[:, :SSM_HEADS], pad, wdt_raw[:, SSM_HEADS:], pad], axis=1).astype(BF16)
    bpad = jnp.zeros((LANES - SSM_HEADS,), F32)
    dtb = jnp.concatenate([dt_bias[0], bpad, dt_bias[1], bpad])[None]
    cw = jnp.transpose(conv_w.reshape(3, n, cc), (1, 0, 2))
    cb = conv_b.reshape(n, 1, cc)
    a = -jnp.exp(a_log.astype(F32))
    a_rows = jnp.concatenate([a, jnp.zeros((2, LANES - SSM_HEADS), F32)], axis=1).reshape(2, 1, LANES)
    dsk = jnp.repeat(d_skip, SSM_HEADDIM)[None]
    return wz, wx, wdt, cw, cb, dtb, a_rows, dsk, norm_w[None], w_out.astype(BF16)


def _ssd_mixer(xc, xl, mod_c, mod_l, nw, wts, tables, ctx_out, tm_l, tm_c):
    a_rows, dsk, norm_w, w_out = wts[6:]
    (csc, csh, cg), (sc, sh, g) = mod_c, mod_l
    zc, xbc_c, dtc = _ssd_project(xc, csc, csh, nw, wts, tm_c)
    zl, xbc_l, dtl = _ssd_project(xl, sc, sh, nw, wts, tm_l)
    zero = jnp.zeros((2, xc.shape[0], SSM_INNER, SSM_STATE), F32)
    yc, s_ctx = _ssd_scan(xbc_c, dtc, a_rows, tables, zero)
    yl, _ = _ssd_scan(xbc_l, dtl, a_rows, tables, s_ctx)
    xl = _ssd_finish(yl, xbc_l, zl, xl, g, dsk, norm_w, w_out, tm_l)
    if ctx_out:
        xc = _ssd_finish(yc, xbc_c, zc, xc, cg, dsk, norm_w, w_out, tm_c)
    return xc, xl


def kernel(x, c, ctx, c_ctx, ada_w, ada_b, norm_mix_w, norm_ffn_w, ffn_w_up, ffn_conv_w, ffn_conv_b, ffn_w_down, gla_w_in, gla_w_g1, gla_w_g2, gla_b_g, gla_norm_w, gla_w_out, mla_w_in, mla_q_a_norm, mla_w_uq, mla_kv_a_norm, mla_w_ukv, mla_q_norm, mla_k_norm, mla_w_out, ssm_w_in, ssm_conv_w, ssm_conv_b, ssm_dt_bias, ssm_a_log, ssm_d, ssm_norm_w, ssm_w_out):
    bsz, l, d = x.shape
    lc = ctx.shape[1]
    tm_l = min(512, l)
    tm_c = lc
    cond = jnp.concatenate([c, c_ctx[None], jnp.zeros((8 - bsz - 1, d), F32)], axis=0)
    mods = _modulation(cond, ada_w, ada_b)
    gla_tables = _gla_tables(SCAN_CHUNK)
    ssd_tables = _ssd_tables(SCAN_CHUNK)
    xl, xc = x, ctx
    for i in range(DEPTH):
        kind, j = i % N_MIXERS, i // N_MIXERS
        ctx_out = i < DEPTH - 1
        m6 = [mods[i, :, k * d:(k + 1) * d] for k in range(N_MOD)]
        lat = [m[:bsz, None, :] for m in m6]
        cx = [m[bsz:bsz + 1, None, :] for m in m6]
        nw = norm_mix_w[i][None]
        mod_l, mod_c = (lat[1], lat[0], lat[2]), (cx[1], cx[0], cx[2])
        if kind == 0:
            wts = _gla_weights(gla_w_in[j], gla_w_g1[j], gla_w_g2[j], gla_b_g[j], gla_norm_w[j], gla_w_out[j])
            xc, xl = _gla_mixer(xc, xl, mod_c, mod_l, nw, wts, gla_tables, ctx_out, tm_l, tm_c)
        elif kind == 1:
            wts = _mla_weights(mla_w_in[j], mla_q_a_norm[j], mla_w_uq[j], mla_kv_a_norm[j], mla_w_ukv[j],
                               mla_q_norm[j], mla_k_norm[j], mla_w_out[j])
            xc, xl = _mla_mixer(xc, xl, mod_c, mod_l, nw, wts, ctx_out, tm_l, tm_c)
        else:
            wts = _ssd_weights(ssm_w_in[j], ssm_conv_w[j], ssm_conv_b[j], ssm_dt_bias[j], ssm_a_log[j], ssm_d[j],
                               ssm_norm_w[j], ssm_w_out[j])
            xc, xl = _ssd_mixer(xc, xl, mod_c, mod_l, nw, wts, ssd_tables, ctx_out, tm_l, tm_c)
        fw = _ffn_weights(ffn_w_up[i], ffn_conv_w[i], ffn_conv_b[i], ffn_w_down[i])
        nfw = norm_ffn_w[i][None]
        xl = _conv_ffn_residual(xl, lat[4], lat[3], lat[5], nfw, fw, tm_l)
        if ctx_out:
            xc = _conv_ffn_residual(xc, cx[4], cx[3], cx[5], nfw, fw, tm_c)
    return xl
```

```python
import functools

import numpy as np
import jax
import jax.numpy as jnp
from jax import lax
from jax.experimental import pallas as pl
from jax.experimental.pallas import tpu as pltpu

F32 = jnp.float32
BF16 = jnp.bfloat16

D_MODEL = 1024
DEPTH = 4
GRID_W = 64
N_MIXERS = 3
N_MOD = 6
EPS = 1e-6

GLA_HEADS = 4
GLA_DK = D_MODEL // 2
GLA_DV = D_MODEL
GLA_HK = GLA_DK // GLA_HEADS
GLA_HV = GLA_DV // GLA_HEADS
GLA_GATE_RANK = 16
GLA_TAU = 16.0
GLA_IN = 2 * GLA_DK + 2 * GLA_DV

MLA_HEADS = 8
MLA_NOPE = 128
MLA_ROPE = 64
MLA_QK = MLA_NOPE + MLA_ROPE
MLA_V = 128
MLA_Q_RANK = 256
MLA_KV_RANK = 128
ROPE_THETA = 10000.0

SSM_INNER = 2 * D_MODEL
SSM_HEADDIM = 64
SSM_HEADS = SSM_INNER // SSM_HEADDIM
SSM_STATE = 128
SSM_GROUPS = 4
SSM_BC = SSM_GROUPS * SSM_STATE
SSM_XBC = SSM_INNER + 2 * SSM_BC

FFN_HIDDEN = 2816

LANES = 128
HALO = 8
VMEM_LIMIT_BYTES = 56 * 1024 * 1024
SCAN_CHUNK = 128
FFN_COL_CHUNK = 256
FFN_ROWS = 1024
SSD_COL_CHUNK = 512


def _params(n_axes):
    return pltpu.CompilerParams(dimension_semantics=("arbitrary",) * n_axes,
                                vmem_limit_bytes=VMEM_LIMIT_BYTES)


def _dot(a, b):
    return jnp.dot(a, b, preferred_element_type=F32)


def _dot_nt(a, b):
    return lax.dot_general(a, b, (((1,), (1,)), ((), ())), preferred_element_type=F32)


def _dot_tn(a, b):
    return lax.dot_general(a, b, (((0,), (0,)), ((), ())), preferred_element_type=F32)


def _split_dot(mat_bf16, v_f32):
    hi = v_f32.astype(BF16)
    lo = (v_f32 - hi.astype(F32)).astype(BF16)
    return _dot(mat_bf16, hi) + _dot(mat_bf16, lo)


def _norm_mod(x, nw, sc, sh):
    ms = jnp.mean(x * x, axis=-1, keepdims=True)
    return (x * lax.rsqrt(ms + EPS) * nw) * (1.0 + sc) + sh


def _silu(x):
    return x * jax.nn.sigmoid(x)


def _mod_index(n_mod):
    if n_mod == 1:
        return lambda b, i: (0, 0, 0)
    return lambda b, i: (b, 0, 0)


def _modulation_kernel(cond_ref, w_ref, b_ref, o_ref):
    a = _silu(cond_ref[...]).astype(BF16)
    o_ref[0] = _dot(a, w_ref[0].astype(BF16)) + b_ref[0]


def _modulation(cond, ada_w, ada_b):
    r, d = cond.shape
    depth, _, n = ada_w.shape
    tn = 512
    return pl.pallas_call(
        _modulation_kernel,
        grid=(depth, n // tn),
        in_specs=[pl.BlockSpec((r, d), lambda l, j: (0, 0)),
                  pl.BlockSpec((1, d, tn), lambda l, j: (l, 0, j)),
                  pl.BlockSpec((1, 1, tn), lambda l, j: (l, 0, j))],
        out_specs=pl.BlockSpec((1, r, tn), lambda l, j: (l, 0, j)),
        out_shape=jax.ShapeDtypeStruct((depth, r, n), F32),
        compiler_params=_params(2),
        name="adaln_modulation",
    )(cond, ada_w, ada_b.reshape(depth, 1, n))


def _halo_specs(tm, l, d):
    per = tm // HALO
    last = l // HALO - 1
    return [pl.BlockSpec((1, tm, d), lambda b, i: (b, i, 0)),
            pl.BlockSpec((1, HALO, d), lambda b, i: (b, jnp.maximum(i * per - 1, 0), 0)),
            pl.BlockSpec((1, HALO, d), lambda b, i: (b, jnp.minimum((i + 1) * per, last), 0))]


def _halo_rows(x_ref, xp_ref, xn_ref, nw, sc, sh):
    i = pl.program_id(1)
    has_prev = (i > 0).astype(F32)
    has_next = (i < pl.num_programs(1) - 1).astype(F32)
    hp = _norm_mod(xp_ref[0], nw, sc, sh) * has_prev
    hm = _norm_mod(x_ref[0], nw, sc, sh)
    hn = _norm_mod(xn_ref[0], nw, sc, sh) * has_next
    return jnp.concatenate([hp, hm, hn], axis=0).astype(BF16)


def _conv3(u_ref, tm, w, b):
    return (u_ref[pl.ds(HALO - 1, tm), :] * w[0:1] + u_ref[pl.ds(HALO, tm), :] * w[1:2]
            + u_ref[pl.ds(HALO + 1, tm), :] * w[2:3] + b)


def _conv3_rows(u, tm, w, b):
    rows = u.shape[0]
    y = pltpu.roll(u, 1, 0) * w[0:1] + u * w[1:2] + pltpu.roll(u, rows - 1, 0) * w[2:3] + b
    return y[HALO:HALO + tm]


def _ffn_kernel(x_ref, xp_ref, xn_ref, sc_ref, sh_ref, g_ref, nw_ref, wg_ref, wv_ref, cwg_ref, cwv_ref,
                cbg_ref, cbv_ref, wd_ref, o_ref, h_buf, a_buf, *, tm, n_chunks, fc):
    h_buf[...] = _halo_rows(x_ref, xp_ref, xn_ref, nw_ref[...], sc_ref[0], sh_ref[0])
    for c in range(n_chunks):
        gate = _conv3_rows(_dot(h_buf[...], wg_ref[c]), tm, cwg_ref[c], cbg_ref[c])
        val = _conv3_rows(_dot(h_buf[...], wv_ref[c]), tm, cwv_ref[c], cbv_ref[c])
        a_buf[:, c * fc:(c + 1) * fc] = (_silu(gate) * val).astype(BF16)
    for n in range(D_MODEL // fc):
        cs = slice(n * fc, (n + 1) * fc)
        o_ref[0, :, cs] = x_ref[0, :, cs] + g_ref[0, :, cs] * _dot(a_buf[...], wd_ref[:, cs])


def _conv_ffn_residual(x, sc, sh, g, nw, wts, tm):
    b, l, d = x.shape
    wg, wv, cwg, cwv, cbg, cbv, wd = wts
    n_chunks, _, fc = wg.shape
    mod = pl.BlockSpec((1, 1, d), _mod_index(sc.shape[0]))
    full3 = lambda a: pl.BlockSpec(a.shape, lambda b_, i: (0, 0, 0), pipeline_mode=pl.Buffered(1))
    kern = functools.partial(_ffn_kernel, tm=tm, n_chunks=n_chunks, fc=fc)
    return pl.pallas_call(
        kern,
        grid=(b, l // tm),
        in_specs=_halo_specs(tm, l, d) + [mod, mod, mod, pl.BlockSpec((1, d), lambda b_, i: (0, 0)),
                                           full3(wg), full3(wv), full3(cwg), full3(cwv), full3(cbg), full3(cbv),
                                           pl.BlockSpec(wd.shape, lambda b_, i: (0, 0), pipeline_mode=pl.Buffered(1))],
        out_specs=pl.BlockSpec((1, tm, d), lambda b_, i: (b_, i, 0)),
        out_shape=jax.ShapeDtypeStruct((b, l, d), F32),
        scratch_shapes=[pltpu.VMEM((tm + 2 * HALO, d), BF16), pltpu.VMEM((tm, n_chunks * fc), BF16)],
        compiler_params=_params(2),
        name="conv_ffn",
    )(x, x, x, sc, sh, g, nw, wg, wv, cwg, cwv, cbg, cbv, wd)


def _ffn_weights(w_up, conv_w, conv_b, w_down):
    f, fc = FFN_HIDDEN, FFN_COL_CHUNK
    n = f // fc
    cols = lambda w: jnp.transpose(w.reshape(w.shape[0], n, fc), (1, 0, 2))
    return (cols(w_up[:, :f]).astype(BF16), cols(w_up[:, f:]).astype(BF16),
            cols(conv_w[:, :f]), cols(conv_w[:, f:]),
            cols(conv_b[None, :f]), cols(conv_b[None, f:]),
            w_down.astype(BF16))


def _out_proj_kernel(a_ref, x_ref, g_ref, w_ref, o_ref):
    o_ref[0] = x_ref[0] + g_ref[0] * _dot(a_ref[0].astype(BF16), w_ref[...])


def _out_proj_residual(a, x, g, w, tm):
    b, l, d = x.shape
    k = a.shape[-1]
    return pl.pallas_call(
        _out_proj_kernel,
        grid=(b, l // tm),
        in_specs=[pl.BlockSpec((1, tm, k), lambda b_, i: (b_, i, 0)),
                  pl.BlockSpec((1, tm, d), lambda b_, i: (b_, i, 0)),
                  pl.BlockSpec((1, 1, d), _mod_index(g.shape[0])),
                  pl.BlockSpec((k, d), lambda b_, i: (0, 0))],
        out_specs=pl.BlockSpec((1, tm, d), lambda b_, i: (b_, i, 0)),
        out_shape=jax.ShapeDtypeStruct((b, l, d), F32),
        compiler_params=_params(2),
        name="out_proj_residual",
    )(a, x, g, w)


GLA_W1_COLS = GLA_IN + LANES


def _gla_proj_kernel(x_ref, sc_ref, sh_ref, nw_ref, w1_ref, w2_ref, bg_ref, qk_ref, vr_ref, lg_ref):
    h = _norm_mod(x_ref[0], nw_ref[...], sc_ref[0], sh_ref[0]).astype(BF16)
    p = _dot(h, w1_ref[...])
    qk_ref[0] = p[:, :2 * GLA_DK]
    vr_ref[0] = p[:, 2 * GLA_DK:GLA_IN].astype(BF16)
    z = _dot(p[:, GLA_IN:].astype(BF16), w2_ref[...]) + bg_ref[...]
    lg_ref[0] = jax.nn.log_sigmoid(z) * (1.0 / GLA_TAU)


def _gla_project(x, sc, sh, nw, w1, w2, bg, tm):
    b, l, d = x.shape
    mod = pl.BlockSpec((1, 1, d), _mod_index(sc.shape[0]))
    const = lambda a: pl.BlockSpec(a.shape, lambda b_, i: (0, 0))
    return pl.pallas_call(
        _gla_proj_kernel,
        grid=(b, l // tm),
        in_specs=[pl.BlockSpec((1, tm, d), lambda b_, i: (b_, i, 0)), mod, mod, const(nw), const(w1), const(w2),
                  const(bg)],
        out_specs=[pl.BlockSpec((1, tm, 2 * GLA_DK), lambda b_, i: (b_, i, 0)),
                   pl.BlockSpec((1, tm, 2 * GLA_DV), lambda b_, i: (b_, i, 0)),
                   pl.BlockSpec((1, tm, 2 * GLA_DK), lambda b_, i: (b_, i, 0))],
        out_shape=[jax.ShapeDtypeStruct((b, l, 2 * GLA_DK), F32), jax.ShapeDtypeStruct((b, l, 2 * GLA_DV), BF16),
                   jax.ShapeDtypeStruct((b, l, 2 * GLA_DK), F32)],
        compiler_params=_params(2),
        name="gla_in_proj",
    )(x, sc, sh, nw, w1, w2, bg)


def _gla_tables(c):
    n_lev = int(np.log2(c))
    idx = np.arange(c)
    i, t = idx[:, None], idx[None, :]
    dec = np.zeros(((n_lev + 2) * c + 8, c), np.float32)
    dec[0:c] = t <= i
    dec[c:2 * c] = t > i
    mask = np.zeros((n_lev + 1, c, c), np.float32)
    mask[0] = np.eye(c)
    for lev in range(n_lev):
        m = c >> (lev + 1)
        mid = (i // (2 * m)) * 2 * m + m
        upper = (i % (2 * m)) >= m
        dec[(2 + lev) * c:(3 + lev) * c] = np.where(upper, (t >= mid) & (t <= i), (t > i) & (t < mid))
        j = t
        mask[lev + 1] = (i // (2 * m) == j // (2 * m)) & upper & ((j % (2 * m)) < m)
    dec[(n_lev + 2) * c:] = 1.0
    rev = lambda a: a[..., ::-1, ::-1]
    dec_b = np.concatenate([np.concatenate([rev(dec[k * c:(k + 1) * c]) for k in range(n_lev + 2)], 0),
                            dec[(n_lev + 2) * c:]], 0)
    return (jnp.asarray(np.stack([dec, dec_b]), BF16), jnp.asarray(np.stack([mask, rev(mask)]), F32))


def _gla_scan_kernel(qkf_ref, qkb_ref, vf_ref, vb_ref, lgf_ref, lgb_ref, dec_ref, mask_ref, s0_ref,
                     of_ref, ob_ref, sT_ref, state, *, c, n_lev):
    @pl.when(pl.program_id(1) == 0)
    def _():
        state[...] = s0_ref[:, 0]

    dirs = ((qkf_ref, vf_ref, lgf_ref, of_ref), (qkb_ref, vb_ref, lgb_ref, ob_ref))
    for d, (qk_ref, v_ref, lg_ref, o_ref) in enumerate(dirs):
        for h in range(GLA_HEADS):
            ks = slice(h * GLA_HK, (h + 1) * GLA_HK)
            kks = slice(GLA_DK + h * GLA_HK, GLA_DK + (h + 1) * GLA_HK)
            vs = slice(h * GLA_HV, (h + 1) * GLA_HV)
            q = qk_ref[0, :, ks] * (GLA_HK ** -0.5)
            k = qk_ref[0, :, kks]
            vb = v_ref[0, :, vs]
            lg = lg_ref[0, :, ks]
            hi = lg.astype(BF16)
            lo = (lg - hi.astype(F32)).astype(BF16)
            d2 = _dot(dec_ref[d], jnp.concatenate([hi, lo], axis=1))
            e_all = jnp.exp(d2[:, :GLA_HK] + d2[:, GLA_HK:])
            att = mask_ref[d, 0] * _dot_nt(q.astype(BF16), k.astype(BF16))
            for lev in range(n_lev):
                e = e_all[(2 + lev) * c:(3 + lev) * c]
                att = att + mask_ref[d, lev + 1] * _dot_nt((q * e).astype(BF16), (k * e).astype(BF16))
            st = state[d, h]
            o = _dot(att.astype(BF16), vb) + _dot_nt((q * e_all[0:c]).astype(BF16), st.astype(BF16))
            o_ref[0, :, vs] = o.astype(o_ref.dtype)
            e_tot = e_all[(n_lev + 2) * c:(n_lev + 2) * c + 1]
            st = st * e_tot + _dot_tn(vb, (k * e_all[c:2 * c]).astype(BF16))
            state[d, h] = st
            sT_ref[d, 0, h] = st


def _gla_scan(qk, vr, lg, tables, s0):
    b, l, _ = qk.shape
    c = SCAN_CHUNK
    n = l // c
    dec, mask = tables
    n_lev = mask.shape[1] - 1
    kern = functools.partial(_gla_scan_kernel, c=c, n_lev=n_lev)
    fwd = lambda blk: (lambda b_, i: (b_, i, blk))
    bwd = lambda blk: (lambda b_, i: (b_, n - 1 - i, blk))
    st_spec = pl.BlockSpec((2, 1, GLA_HEADS, GLA_HV, GLA_HK), lambda b_, i: (0, b_, 0, 0, 0))
    return pl.pallas_call(
        kern,
        grid=(b, n),
        in_specs=[pl.BlockSpec((1, c, 2 * GLA_DK), fwd(0)), pl.BlockSpec((1, c, 2 * GLA_DK), bwd(0)),
                  pl.BlockSpec((1, c, GLA_DV), fwd(0)), pl.BlockSpec((1, c, GLA_DV), bwd(0)),
                  pl.BlockSpec((1, c, GLA_DK), fwd(0)), pl.BlockSpec((1, c, GLA_DK), bwd(1)),
                  pl.BlockSpec(dec.shape, lambda b_, i: (0, 0, 0)),
                  pl.BlockSpec(mask.shape, lambda b_, i: (0, 0, 0, 0)),
                  st_spec],
        out_specs=[pl.BlockSpec((1, c, GLA_DV), fwd(0)), pl.BlockSpec((1, c, GLA_DV), bwd(0)), st_spec],
        out_shape=[jax.ShapeDtypeStruct((b, l, GLA_DV), BF16), jax.ShapeDtypeStruct((b, l, GLA_DV), BF16),
                   jax.ShapeDtypeStruct((2, b, GLA_HEADS, GLA_HV, GLA_HK), F32)],
        scratch_shapes=[pltpu.VMEM((2, GLA_HEADS, GLA_HV, GLA_HK), F32)],
        compiler_params=_params(2),
        name="gla_scan",
    )(qk, qk, vr, vr, lg, lg, dec, mask, s0)


def _gla_finish_kernel(of_ref, ob_ref, r_ref, x_ref, g_ref, nw_ref, w_ref, o_ref):
    o = of_ref[0].astype(F32) + ob_ref[0].astype(F32)
    heads = []
    for h in range(GLA_HEADS):
        oh = o[:, h * GLA_HV:(h + 1) * GLA_HV]
        ms = jnp.mean(oh * oh, axis=-1, keepdims=True)
        heads.append(oh * lax.rsqrt(ms + EPS) * nw_ref[...])
    y = jnp.concatenate(heads, axis=-1) * _silu(r_ref[0].astype(F32))
    o_ref[0] = x_ref[0] + g_ref[0] * _dot(y.astype(BF16), w_ref[...])


def _gla_finish(o_f, o_b, vr, x, g, nw, w, tm):
    b, l, d = x.shape
    return pl.pallas_call(
        _gla_finish_kernel,
        grid=(b, l // tm),
        in_specs=[pl.BlockSpec((1, tm, GLA_DV), lambda b_, i: (b_, i, 0)),
                  pl.BlockSpec((1, tm, GLA_DV), lambda b_, i: (b_, i, 0)),
                  pl.BlockSpec((1, tm, GLA_DV), lambda b_, i: (b_, i, 1)),
                  pl.BlockSpec((1, tm, d), lambda b_, i: (b_, i, 0)),
                  pl.BlockSpec((1, 1, d), _mod_index(g.shape[0])),
                  pl.BlockSpec((1, GLA_HV), lambda b_, i: (0, 0)),
                  pl.BlockSpec((GLA_DV, d), lambda b_, i: (0, 0))],
        out_specs=pl.BlockSpec((1, tm, d), lambda b_, i: (b_, i, 0)),
        out_shape=jax.ShapeDtypeStruct((b, l, d), F32),
        compiler_params=_params(2),
        name="gla_finish",
    )(o_f, o_b, vr, x, g, nw, w)


def _gla_weights(w_in, w_g1, w_g2, b_g, norm_w, w_out):
    d = w_in.shape[0]
    r = GLA_GATE_RANK
    w1 = jnp.concatenate([w_in, w_g1[0], w_g1[1], jnp.zeros((d, LANES - 2 * r), F32)], axis=1).astype(BF16)
    w2 = jnp.zeros((LANES, 2 * GLA_DK), F32)
    w2 = w2.at[0:r, :GLA_DK].set(w_g2[0]).at[r:2 * r, GLA_DK:].set(w_g2[1]).astype(BF16)
    return w1, w2, b_g.reshape(1, 2 * GLA_DK), norm_w.reshape(1, GLA_HV), w_out.astype(BF16)


def _gla_mixer(xc, xl, mod_c, mod_l, nw, wts, tables, ctx_out, tm_l, tm_c):
    w1, w2, bg, norm_w, w_out = wts
    (csc, csh, cg), (sc, sh, g) = mod_c, mod_l
    qkc, vrc, lgc = _gla_project(xc, csc, csh, nw, w1, w2, bg, tm_c)
    qkl, vrl, lgl = _gla_project(xl, sc, sh, nw, w1, w2, bg, tm_l)
    zero = jnp.zeros((2, xc.shape[0], GLA_HEADS, GLA_HV, GLA_HK), F32)
    ocf, ocb, s_ctx = _gla_scan(qkc, vrc, lgc, tables, zero)
    olf, olb, _ = _gla_scan(qkl, vrl, lgl, tables, s_ctx)
    xl = _gla_finish(olf, olb, vrl, xl, g, norm_w, w_out, tm_l)
    if ctx_out:
        xc = _gla_finish(ocf, ocb, vrc, xc, cg, norm_w, w_out, tm_c)
    return xc, xl


MLA_HEAD_PAD = 2 * LANES
MLA_V_PAD = 2 * LANES


def _rot_cols(w):
    half = MLA_ROPE // 2
    return jnp.concatenate([-w[..., half:], w[..., :half]], axis=-1)


def _swap_halves(w):
    half = MLA_ROPE // 2
    return jnp.concatenate([w[..., half:], w[..., :half]], axis=-1)


def _mla_proj_kernel(x_ref, sc_ref, sh_ref, nw_ref, win_ref, qan_ref, kvan_ref, wuq_ref, wukv_ref,
                     qna_ref, qnb_ref, kna_ref, knb_ref, cs_ref, q_ref, k_ref, v_ref):
    h = _norm_mod(x_ref[0], nw_ref[...], sc_ref[0], sh_ref[0]).astype(BF16)
    p = _dot(h, win_ref[...])
    cq = p[:, :MLA_Q_RANK]
    ckv = p[:, MLA_Q_RANK:MLA_Q_RANK + MLA_KV_RANK]
    kpe = p[:, MLA_Q_RANK + MLA_KV_RANK:]
    cq = cq * lax.rsqrt(jnp.mean(cq * cq, axis=-1, keepdims=True) + EPS) * qan_ref[...]
    ckv = ckv * lax.rsqrt(jnp.mean(ckv * ckv, axis=-1, keepdims=True) + EPS) * kvan_ref[...]
    qall = _dot(cq.astype(BF16), wuq_ref[...])
    kvall = _dot(ckv.astype(BF16), wukv_ref[...])
    cs = cs_ref[...]
    lane = lax.broadcasted_iota(jnp.int32, (1, LANES), 1)
    keep = (lane < MLA_ROPE).astype(F32)
    kpe_ss = 0.5 * jnp.sum(kpe * kpe, axis=-1, keepdims=True)
    scale = MLA_QK ** -0.5 * np.log2(np.e)

    def head(nope, pe, na, nb, mult):
        ss = jnp.sum(nope * nope, axis=-1, keepdims=True) + 0.5 * jnp.sum(pe * pe, axis=-1, keepdims=True)
        s = lax.rsqrt(ss * (1.0 / MLA_QK) + EPS)
        t = pe * s * nb * cs
        rope = (t + pltpu.roll(t, MLA_ROPE, 1)) * keep
        return jnp.concatenate([nope * s * na * mult, rope * mult], axis=-1).astype(BF16)

    for hd in range(MLA_HEADS):
        qh = qall[:, hd * MLA_HEAD_PAD:(hd + 1) * MLA_HEAD_PAD]
        q_ref[0, hd] = head(qh[:, :LANES], qh[:, LANES:], qna_ref[...], qnb_ref[...], scale)
        kh = kvall[:, hd * 2 * LANES:hd * 2 * LANES + LANES]
        ss = jnp.sum(kh * kh, axis=-1, keepdims=True) + kpe_ss
        s = lax.rsqrt(ss * (1.0 / MLA_QK) + EPS)
        t = kpe * s * knb_ref[...] * cs
        rope = (t + pltpu.roll(t, MLA_ROPE, 1)) * keep
        k_ref[0, hd] = jnp.concatenate([kh * s * kna_ref[...], rope], axis=-1).astype(BF16)
        vh = kvall[:, hd * 2 * LANES + LANES:(hd + 1) * 2 * LANES]
        v_ref[0, hd] = jnp.concatenate([vh, jnp.ones_like(vh)], axis=-1).astype(BF16)


def _mla_project(x, sc, sh, nw, wts, cs, tm):
    b, l, d = x.shape
    win, qan, kvan, wuq, wukv, qna, qnb, kna, knb = wts[:9]
    mod = pl.BlockSpec((1, 1, d), _mod_index(sc.shape[0]))
    const = lambda a: pl.BlockSpec(a.shape, lambda b_, i: (0, 0))
    cs_spec = (pl.BlockSpec((tm, LANES), lambda b_, i: (i, 0)) if cs.shape[0] == l
               else pl.BlockSpec((tm, LANES), lambda b_, i: (0, 0)))
    hspec = lambda w: pl.BlockSpec((1, MLA_HEADS, tm, w), lambda b_, i: (b_, 0, i, 0))
    return pl.pallas_call(
        _mla_proj_kernel,
        grid=(b, l // tm),
        in_specs=[pl.BlockSpec((1, tm, d), lambda b_, i: (b_, i, 0)), mod, mod, const(nw), const(win), const(qan),
                  const(kvan), const(wuq), const(wukv), const(qna), const(qnb), const(kna), const(knb), cs_spec],
        out_specs=[hspec(MLA_HEAD_PAD), hspec(MLA_HEAD_PAD), hspec(MLA_V_PAD)],
        out_shape=[jax.ShapeDtypeStruct((b, MLA_HEADS, l, MLA_HEAD_PAD), BF16),
                   jax.ShapeDtypeStruct((b, MLA_HEADS, l, MLA_HEAD_PAD), BF16),
                   jax.ShapeDtypeStruct((b, MLA_HEADS, l, MLA_V_PAD), BF16)],
        compiler_params=_params(2),
        name="mla_project",
    )(x, sc, sh, nw, win, qan, kvan, wuq, wukv, qna, qnb, kna, knb, cs)


def _flash_kernel(*refs, tk, n_kv, has_ctx):
    if has_ctx:
        q_ref, k_ref, v_ref, kc_ref, vc_ref, o_ref = refs
    else:
        q_ref, k_ref, v_ref, o_ref = refs
    q = q_ref[0, 0]

    def update(carry, kj, vj):
        m, acc = carry
        s = _dot_nt(q, kj)
        m_new = jnp.maximum(m, jnp.max(s, axis=-1, keepdims=True))
        p = jnp.exp2(s - m_new)
        return m_new, jnp.exp2(m - m_new) * acc + _dot(p.astype(BF16), vj)

    tq = q.shape[0]
    carry = (jnp.full((tq, 1), -jnp.inf, F32), jnp.zeros((tq, MLA_V_PAD), F32))
    if has_ctx:
        carry = update(carry, kc_ref[0, 0], vc_ref[0, 0])
    for j in range(n_kv):
        carry = update(carry, k_ref[0, 0, j * tk:(j + 1) * tk, :], v_ref[0, 0, j * tk:(j + 1) * tk, :])
    acc = carry[1]
    o_ref[0] = (acc[:, :MLA_V] / acc[:, MLA_V:]).astype(o_ref.dtype)


def _flash_attention(q, k, v, kc, vc, tq, tk):
    b, hds, l, dq = q.shape
    lk = k.shape[2]
    has_ctx = kc is not None
    kern = functools.partial(_flash_kernel, tk=tk, n_kv=lk // tk, has_ctx=has_ctx)
    whole = lambda a: pl.BlockSpec((1, 1) + a.shape[2:], lambda b_, h, i: (b_, h, 0, 0))
    in_specs = [pl.BlockSpec((1, 1, tq, dq), lambda b_, h, i: (b_, h, i, 0)), whole(k), whole(v)]
    args = [q, k, v]
    if has_ctx:
        in_specs += [whole(kc), whole(vc)]
        args += [kc, vc]
    return pl.pallas_call(
        kern,
        grid=(b, hds, l // tq),
        in_specs=in_specs,
        out_specs=pl.BlockSpec((1, tq, MLA_V), lambda b_, h, i: (b_, i, h)),
        out_shape=jax.ShapeDtypeStruct((b, l, hds * MLA_V), BF16),
        compiler_params=_params(3),
        name="mla_flash_attention",
    )(*args)


def _mla_weights(w_in, q_a_norm, w_uq, kv_a_norm, w_ukv, q_norm, k_norm, w_out):
    kpe = w_in[:, MLA_Q_RANK + MLA_KV_RANK:]
    win = jnp.concatenate([w_in, _rot_cols(kpe)], axis=1).astype(BF16)
    wq = w_uq.reshape(MLA_Q_RANK, MLA_HEADS, MLA_QK)
    wq = jnp.concatenate([wq, _rot_cols(wq[..., MLA_NOPE:])], axis=-1)
    wuq = wq.reshape(MLA_Q_RANK, MLA_HEADS * MLA_HEAD_PAD).astype(BF16)
    gains = lambda nrm: (nrm[None, :MLA_NOPE],
                         jnp.concatenate([nrm[MLA_NOPE:], _swap_halves(nrm[MLA_NOPE:])])[None])
    qna, qnb = gains(q_norm)
    kna, knb = gains(k_norm)
    return (win, q_a_norm[None], kv_a_norm[None], wuq, w_ukv.astype(BF16), qna, qnb, kna, knb, w_out.astype(BF16))


def _rope_table(l):
    rows = l // GRID_W
    pos_r = jnp.repeat(jnp.arange(rows, dtype=F32), GRID_W)
    pos_c = jnp.tile(jnp.arange(GRID_W, dtype=F32), rows)
    n_freq = MLA_ROPE // 4
    inv = ROPE_THETA ** (-jnp.arange(n_freq, dtype=F32) / n_freq)
    ang = jnp.concatenate([pos_r[:, None] * inv, pos_c[:, None] * inv], axis=-1)
    return jnp.concatenate([jnp.cos(ang), jnp.cos(ang), jnp.sin(ang), jnp.sin(ang)], axis=-1)


def _mla_mixer(xc, xl, mod_c, mod_l, nw, wts, ctx_out, tm_l, tm_c):
    (csc, csh, cg), (sc, sh, g) = mod_c, mod_l
    w_out = wts[9]
    lc, l = xc.shape[1], xl.shape[1]
    no_rope = jnp.concatenate([jnp.ones((tm_c, LANES // 2), F32), jnp.zeros((tm_c, LANES // 2), F32)], axis=-1)
    qc, kc, vc = _mla_project(xc, csc, csh, nw, wts, no_rope, tm_c)
    ql, kl, vl = _mla_project(xl, sc, sh, nw, wts, _rope_table(l), tm_l)
    tq = min(1024, l)
    al = _flash_attention(ql, kl, vl, kc, vc, tq, min(512, l))
    xl = _out_proj_residual(al, xl, g, w_out, tm_l)
    if ctx_out:
        ac = _flash_attention(qc, kc, vc, None, None, lc, lc)
        xc = _out_proj_residual(ac, xc, cg, w_out, tm_c)
    return xc, xl


def _ssd_proj_kernel(x_ref, xp_ref, xn_ref, sc_ref, sh_ref, nw_ref, wz_ref, wx_ref, wdt_ref, cw_ref, cb_ref,
                     dtb_ref, z_ref, xbc_ref, dt_ref, h_buf, *, tm, n_chunks, cc):
    h_buf[...] = _halo_rows(x_ref, xp_ref, xn_ref, nw_ref[...], sc_ref[0], sh_ref[0])
    hm = h_buf[pl.ds(HALO, tm), :]
    z_ref[0] = _dot(hm, wz_ref[...]).astype(z_ref.dtype)
    dt = jax.nn.softplus(_dot(hm, wdt_ref[...]) + dtb_ref[...])
    dt_ref[0, 0] = dt[:, :LANES]
    dt_ref[1, 0] = dt[:, LANES:]
    for c in range(n_chunks):
        u = _conv3_rows(_dot(h_buf[...], wx_ref[c]), tm, cw_ref[c], cb_ref[c])
        xbc_ref[0, :, c * cc:(c + 1) * cc] = _silu(u).astype(xbc_ref.dtype)


def _ssd_project(x, sc, sh, nw, wts, tm):
    b, l, d = x.shape
    wz, wx, wdt, cw, cb, dtb = wts[:6]
    n_chunks, _, cc = wx.shape
    mod = pl.BlockSpec((1, 1, d), _mod_index(sc.shape[0]))
    const2 = lambda a: pl.BlockSpec(a.shape, lambda b_, i: (0, 0))
    const3 = lambda a: pl.BlockSpec(a.shape, lambda b_, i: (0, 0, 0))
    kern = functools.partial(_ssd_proj_kernel, tm=tm, n_chunks=n_chunks, cc=cc)
    return pl.pallas_call(
        kern,
        grid=(b, l // tm),
        in_specs=_halo_specs(tm, l, d) + [mod, mod, const2(nw), const2(wz), const3(wx), const2(wdt), const3(cw),
                                           const3(cb), const2(dtb)],
        out_specs=[pl.BlockSpec((1, tm, SSM_INNER), lambda b_, i: (b_, i, 0)),
                   pl.BlockSpec((1, tm, SSM_XBC), lambda b_, i: (b_, i, 0)),
                   pl.BlockSpec((2, 1, tm, LANES), lambda b_, i: (0, b_, i, 0))],
        out_shape=[jax.ShapeDtypeStruct((b, l, SSM_INNER), BF16),
                   jax.ShapeDtypeStruct((b, l, SSM_XBC), BF16),
                   jax.ShapeDtypeStruct((2, b, l, LANES), F32)],
        scratch_shapes=[pltpu.VMEM((tm + 2 * HALO, d), BF16)],
        compiler_params=_params(2),
        name="ssd_in_proj_conv",
    )(x, x, x, sc, sh, nw, wz, wx, wdt, cw, cb, dtb)


def _ssd_tables(c):
    idx = np.arange(c)
    i, t = idx[:, None], idx[None, :]
    ones = np.ones((8, c), np.float32)
    fwd = np.concatenate([(t <= i).astype(np.float32), ones], 0)
    bwd = np.concatenate([(t >= i).astype(np.float32), ones], 0)
    mask = np.stack([(t <= i), (t >= i)]).astype(np.float32)
    h = np.arange(SSM_HEADS)
    e_seg = np.zeros((LANES, SSM_HEADS, c), np.float32)
    e_seg[h, h, :] = 1.0
    e_seg[SSM_HEADS + h, h, :] = 1.0
    e_x = np.zeros((LANES, 2, SSM_HEADS, SSM_HEADDIM), np.float32)
    for q in range(2):
        e_x[2 * q * SSM_HEADS + h, q, h, :] = 1.0
        e_x[(2 * q + 1) * SSM_HEADS + h, q, h, :] = 1.0
    return (jnp.asarray(np.stack([fwd, bwd]), BF16), jnp.asarray(mask, F32),
            jnp.asarray(e_seg.reshape(LANES, SSM_HEADS * c), BF16),
            jnp.asarray(e_x.reshape(LANES, 2 * SSM_INNER), BF16))


def _pack_hi_lo(parts):
    out = None
    for n, v in enumerate(parts):
        hi = v.astype(BF16).astype(F32)
        for j, piece in enumerate((hi, v - hi)):
            shift = (2 * n + j) * SSM_HEADS
            piece = piece if shift == 0 else pltpu.roll(piece, shift, 1)
            out = piece if out is None else out + piece
    return out.astype(BF16)


def _ssd_scan_kernel(xf_ref, bf_ref, cf_ref, dtf_ref, xb_ref, bb_ref, cb_ref, dtb_ref, a_ref, tri_ref, mask_ref,
                     eseg_ref, ex_ref, s0_ref, yf_ref, yb_ref, sf_ref, state, *, c):
    @pl.when(pl.program_id(1) == 0)
    def _():
        state[...] = s0_ref[:, 0]

    lane = lax.broadcasted_iota(jnp.int32, (1, LANES), 1)
    valid = (lane < SSM_HEADS).astype(F32)
    left = lane < SSM_HEADDIM
    per_group = SSM_HEADS // SSM_GROUPS
    gw = per_group * SSM_HEADDIM
    dirs = ((xf_ref, bf_ref, cf_ref, dtf_ref, yf_ref), (xb_ref, bb_ref, cb_ref, dtb_ref, yb_ref))
    for d, (x_ref, b_ref, c_ref, dt_ref, y_ref) in enumerate(dirs):
        dt = dt_ref[0, 0] * valid
        cums = _split_dot(tri_ref[d], dt * a_ref[d])
        cum = cums[:c]
        total = cums[c:c + 1]
        e_in = jnp.exp(cum) * valid
        w_out = jnp.exp(total - cum) * dt
        e_tot = jnp.exp(total)
        cum_col = _dot(_pack_hi_lo([cum]), eseg_ref[...])
        xs = _dot(_pack_hi_lo([e_in, w_out]), ex_ref[...])
        cum_t = jnp.transpose(cum)
        dt_t = jnp.transpose(dt)
        causal = mask_ref[d] > 0.5
        for g in range(SSM_GROUPS):
            bm = b_ref[0, :, g * SSM_STATE:(g + 1) * SSM_STATE]
            cm = c_ref[0, :, g * SSM_STATE:(g + 1) * SSM_STATE]
            scores = _dot_nt(cm, bm)
            sg = state[d, g * gw:(g + 1) * gw, :]
            y_in = _dot_nt(cm, sg.astype(BF16))
            x_g = x_ref[0, :, g * gw:(g + 1) * gw]
            for pr in range(per_group // 2):
                h0 = g * per_group + 2 * pr
                col = g * gw + pr * LANES
                ms = []
                for h in (h0, h0 + 1):
                    seg = jnp.exp(cum_col[:, h * c:(h + 1) * c] - cum_t[h:h + 1, :])
                    ms.append((scores * jnp.where(causal, seg, 0.0) * dt_t[h:h + 1, :]).astype(BF16))
                xp = x_g[:, pr * LANES:(pr + 1) * LANES]
                zero = jnp.zeros_like(xp)
                xbd = jnp.concatenate([jnp.where(left, xp, zero), jnp.where(left, zero, xp)], axis=0)
                y = _dot(jnp.concatenate(ms, axis=1), xbd)
                y = y + xs[:, col:col + LANES] * y_in[:, pr * LANES:(pr + 1) * LANES]
                y_ref[0, :, col:col + LANES] = y.astype(y_ref.dtype)
            wx = (x_g.astype(F32) * xs[:, SSM_INNER + g * gw:SSM_INNER + (g + 1) * gw]).astype(BF16)
            upd = _dot_tn(wx, bm)
            dec = jnp.concatenate(
                [jnp.broadcast_to(e_tot[:, g * per_group + j:g * per_group + j + 1], (SSM_HEADDIM, SSM_STATE))
                 for j in range(per_group)], axis=0)
            state[d, g * gw:(g + 1) * gw, :] = sg * dec + upd
    sf_ref[:, 0] = state[...]


def _ssd_scan(xbc, dt2, a_rows, tables, s0):
    b, l, _ = xbc.shape
    c = SCAN_CHUNK
    n = l // c
    tri, mask, e_seg, e_x = tables
    xblk = SSM_INNER // SSM_BC
    kern = functools.partial(_ssd_scan_kernel, c=c)
    fwd = lambda blk: (lambda b_, i: (b_, i, blk))
    bwd = lambda blk: (lambda b_, i: (b_, n - 1 - i, blk))
    stream = lambda at: [pl.BlockSpec((1, c, SSM_INNER), at(0)), pl.BlockSpec((1, c, SSM_BC), at(xblk)),
                         pl.BlockSpec((1, c, SSM_BC), at(xblk + 1))]
    whole = lambda a: pl.BlockSpec(a.shape, lambda b_, i: (0,) * a.ndim)
    st_spec = pl.BlockSpec((2, 1, SSM_INNER, SSM_STATE), lambda b_, i: (0, b_, 0, 0))
    return pl.pallas_call(
        kern,
        grid=(b, n),
        in_specs=(stream(fwd) + [pl.BlockSpec((1, 1, c, LANES), lambda b_, i: (0, b_, i, 0))]
                  + stream(bwd) + [pl.BlockSpec((1, 1, c, LANES), lambda b_, i: (1, b_, n - 1 - i, 0))]
                  + [whole(a_rows), whole(tri), whole(mask), whole(e_seg), whole(e_x), st_spec]),
        out_specs=[pl.BlockSpec((1, c, SSM_INNER), fwd(0)), pl.BlockSpec((1, c, SSM_INNER), bwd(0)), st_spec],
        out_shape=[jax.ShapeDtypeStruct((b, l, SSM_INNER), BF16), jax.ShapeDtypeStruct((b, l, SSM_INNER), BF16),
                   jax.ShapeDtypeStruct((2, b, SSM_INNER, SSM_STATE), F32)],
        scratch_shapes=[pltpu.VMEM((2, SSM_INNER, SSM_STATE), F32)],
        compiler_params=_params(2),
        name="ssd_scan",
    )(xbc, xbc, xbc, dt2, xbc, xbc, xbc, dt2, a_rows, tri, mask, e_seg, e_x, s0)


def _ssd_finish_kernel(yf_ref, yb_ref, xs_ref, z_ref, x_ref, g_ref, dsk_ref, nw_ref, w_ref, o_ref):
    y = yf_ref[0].astype(F32) + yb_ref[0].astype(F32) + xs_ref[0].astype(F32) * dsk_ref[...]
    y = y * _silu(z_ref[0].astype(F32))
    gw = SSM_INNER // SSM_GROUPS
    parts = []
    for gi in range(SSM_GROUPS):
        yg = y[:, gi * gw:(gi + 1) * gw]
        ms = jnp.mean(yg * yg, axis=-1, keepdims=True)
        parts.append(yg * lax.rsqrt(ms + EPS) * nw_ref[:, gi * gw:(gi + 1) * gw])
    yn = jnp.concatenate(parts, axis=-1)
    o_ref[0] = x_ref[0] + g_ref[0] * _dot(yn.astype(BF16), w_ref[...])


def _ssd_finish(y_f, y_b, xbc, z, x, g, dsk, nw, w, tm):
    b, l, d = x.shape
    return pl.pallas_call(
        _ssd_finish_kernel,
        grid=(b, l // tm),
        in_specs=[pl.BlockSpec((1, tm, SSM_INNER), lambda b_, i: (b_, i, 0)),
                  pl.BlockSpec((1, tm, SSM_INNER), lambda b_, i: (b_, i, 0)),
                  pl.BlockSpec((1, tm, SSM_INNER), lambda b_, i: (b_, i, 0)),
                  pl.BlockSpec((1, tm, SSM_INNER), lambda b_, i: (b_, i, 0)),
                  pl.BlockSpec((1, tm, d), lambda b_, i: (b_, i, 0)),
                  pl.BlockSpec((1, 1, d), _mod_index(g.shape[0])),
                  pl.BlockSpec((1, SSM_INNER), lambda b_, i: (0, 0)),
                  pl.BlockSpec((1, SSM_INNER), lambda b_, i: (0, 0)),
                  pl.BlockSpec((SSM_INNER, d), lambda b_, i: (0, 0))],
        out_specs=pl.BlockSpec((1, tm, d), lambda b_, i: (b_, i, 0)),
        out_shape=jax.ShapeDtypeStruct((b, l, d), F32),
        compiler_params=_params(2),
        name="ssd_finish",
    )(y_f, y_b, xbc, z, x, g, dsk, nw, w)


def _ssd_weights(w_in, conv_w, conv_b, dt_bias, a_log, d_skip, norm_w, w_out):
    d = w_in.shape[0]
    cc = SSD_COL_CHUNK
    n = SSM_XBC // cc
    wz = w_in[:, :SSM_INNER].astype(BF16)
    wx = jnp.transpose(w_in[:, SSM_INNER:SSM_INNER + SSM_XBC].reshape(d, n, cc), (1, 0, 2)).astype(BF16)
    wdt_raw = w_in[:, SSM_INNER + SSM_XBC:]
    pad = jnp.zeros((d, LANES - SSM_HEADS), F32)
    wdt = jnp.concatenate([wdt_raw[:, :SSM_HEADS], pad, wdt_raw[:, SSM_HEADS:], pad], axis=1).astype(BF16)
    bpad = jnp.zeros((LANES - SSM_HEADS,), F32)
    dtb = jnp.concatenate([dt_bias[0], bpad, dt_bias[1], bpad])[None]
    cw = jnp.transpose(conv_w.reshape(3, n, cc), (1, 0, 2))
    cb = conv_b.reshape(n, 1, cc)
    a = -jnp.exp(a_log.astype(F32))
    a_rows = jnp.concatenate([a, jnp.zeros((2, LANES - SSM_HEADS), F32)], axis=1).reshape(2, 1, LANES)
    dsk = jnp.repeat(d_skip, SSM_HEADDIM)[None]
    return wz, wx, wdt, cw, cb, dtb, a_rows, dsk, norm_w[None], w_out.astype(BF16)


def _ssd_mixer(xc, xl, mod_c, mod_l, nw, wts, tables, ctx_out, tm_l, tm_c):
    a_rows, dsk, norm_w, w_out = wts[6:]
    (csc, csh, cg), (sc, sh, g) = mod_c, mod_l
    zc, xbc_c, dtc = _ssd_project(xc, csc, csh, nw, wts, tm_c)
    zl, xbc_l, dtl = _ssd_project(xl, sc, sh, nw, wts, tm_l)
    zero = jnp.zeros((2, xc.shape[0], SSM_INNER, SSM_STATE), F32)
    ycf, ycb, s_ctx = _ssd_scan(xbc_c, dtc, a_rows, tables, zero)
    ylf, ylb, _ = _ssd_scan(xbc_l, dtl, a_rows, tables, s_ctx)
    xl = _ssd_finish(ylf, ylb, xbc_l, zl, xl, g, dsk, norm_w, w_out, tm_l)
    if ctx_out:
        xc = _ssd_finish(ycf, ycb, xbc_c, zc, xc, cg, dsk, norm_w, w_out, tm_c)
    return xc, xl


def kernel(x, c, ctx, c_ctx, ada_w, ada_b, norm_mix_w, norm_ffn_w, ffn_w_up, ffn_conv_w, ffn_conv_b, ffn_w_down, gla_w_in, gla_w_g1, gla_w_g2, gla_b_g, gla_norm_w, gla_w_out, mla_w_in, mla_q_a_norm, mla_w_uq, mla_kv_a_norm, mla_w_ukv, mla_q_norm, mla_k_norm, mla_w_out, ssm_w_in, ssm_conv_w, ssm_conv_b, ssm_dt_bias, ssm_a_log, ssm_d, ssm_norm_w, ssm_w_out):
    bsz, l, d = x.shape
    lc = ctx.shape[1]
    tm_l = min(512, l)
    tm_c = lc
    cond = jnp.concatenate([c, c_ctx[None], jnp.zeros((8 - bsz - 1, d), F32)], axis=0)
    mods = _modulation(cond, ada_w, ada_b)
    gla_tables = _gla_tables(SCAN_CHUNK)
    ssd_tables = _ssd_tables(SCAN_CHUNK)
    xl, xc = x, ctx
    for i in range(DEPTH):
        kind, j = i % N_MIXERS, i // N_MIXERS
        ctx_out = i < DEPTH - 1
        m6 = [mods[i, :, k * d:(k + 1) * d] for k in range(N_MOD)]
        lat = [m[:bsz, None, :] for m in m6]
        cx = [m[bsz:bsz + 1, None, :] for m in m6]
        nw = norm_mix_w[i][None]
        mod_l, mod_c = (lat[1], lat[0], lat[2]), (cx[1], cx[0], cx[2])
        if kind == 0:
            wts = _gla_weights(gla_w_in[j], gla_w_g1[j], gla_w_g2[j], gla_b_g[j], gla_norm_w[j], gla_w_out[j])
            xc, xl = _gla_mixer(xc, xl, mod_c, mod_l, nw, wts, gla_tables, ctx_out, tm_l, tm_c)
        elif kind == 1:
            wts = _mla_weights(mla_w_in[j], mla_q_a_norm[j], mla_w_uq[j], mla_kv_a_norm[j], mla_w_ukv[j],
                               mla_q_norm[j], mla_k_norm[j], mla_w_out[j])
            xc, xl = _mla_mixer(xc, xl, mod_c, mod_l, nw, wts, ctx_out, tm_l, tm_c)
        else:
            wts = _ssd_weights(ssm_w_in[j], ssm_conv_w[j], ssm_conv_b[j], ssm_dt_bias[j], ssm_a_log[j], ssm_d[j],
                               ssm_norm_w[j], ssm_w_out[j])
            xc, xl = _ssd_mixer(xc, xl, mod_c, mod_l, nw, wts, ssd_tables, ctx_out, tm_l, tm_c)
        fw = _ffn_weights(ffn_w_up[i], ffn_conv_w[i], ffn_conv_b[i], ffn_w_down[i])
        nfw = norm_ffn_w[i][None]
        xl = _conv_ffn_residual(xl, lat[4], lat[3], lat[5], nfw, fw, min(FFN_ROWS, l))
        if ctx_out:
            xc = _conv_ffn_residual(xc, cx[4], cx[3], cx[5], nfw, fw, tm_c)
    return xl
```

```python
import functools

import numpy as np
import jax
import jax.numpy as jnp
from jax import lax
from jax.experimental import pallas as pl
from jax.experimental.pallas import tpu as pltpu

F32 = jnp.float32
BF16 = jnp.bfloat16

D_MODEL = 1024
DEPTH = 4
GRID_W = 64
N_MIXERS = 3
N_MOD = 6
EPS = 1e-6

GLA_HEADS = 4
GLA_DK = D_MODEL // 2
GLA_DV = D_MODEL
GLA_HK = GLA_DK // GLA_HEADS
GLA_HV = GLA_DV // GLA_HEADS
GLA_GATE_RANK = 16
GLA_TAU = 16.0
GLA_IN = 2 * GLA_DK + 2 * GLA_DV

MLA_HEADS = 8
MLA_NOPE = 128
MLA_ROPE = 64
MLA_QK = MLA_NOPE + MLA_ROPE
MLA_V = 128
MLA_Q_RANK = 256
MLA_KV_RANK = 128
ROPE_THETA = 10000.0

SSM_INNER = 2 * D_MODEL
SSM_HEADDIM = 64
SSM_HEADS = SSM_INNER // SSM_HEADDIM
SSM_STATE = 128
SSM_GROUPS = 4
SSM_BC = SSM_GROUPS * SSM_STATE
SSM_XBC = SSM_INNER + 2 * SSM_BC

FFN_HIDDEN = 2816

LANES = 128
HALO = 8
VMEM_LIMIT_BYTES = 56 * 1024 * 1024
SCAN_CHUNK = 128
FFN_COL_CHUNK = 256
FFN_ROWS = 1024
SSD_COL_CHUNK = 512


def _params(n_axes):
    return pltpu.CompilerParams(dimension_semantics=("arbitrary",) * n_axes,
                                vmem_limit_bytes=VMEM_LIMIT_BYTES)


def _dot(a, b):
    return jnp.dot(a, b, preferred_element_type=F32)


def _dot_nt(a, b):
    return lax.dot_general(a, b, (((1,), (1,)), ((), ())), preferred_element_type=F32)


def _dot_tn(a, b):
    return lax.dot_general(a, b, (((0,), (0,)), ((), ())), preferred_element_type=F32)


def _split_dot(mat_bf16, v_f32):
    hi = v_f32.astype(BF16)
    lo = (v_f32 - hi.astype(F32)).astype(BF16)
    return _dot(mat_bf16, hi) + _dot(mat_bf16, lo)


def _norm_mod(x, nw, sc, sh):
    ms = jnp.mean(x * x, axis=-1, keepdims=True)
    return (x * lax.rsqrt(ms + EPS) * nw) * (1.0 + sc) + sh


def _silu(x):
    return x * jax.nn.sigmoid(x)


def _mod_index(n_mod):
    if n_mod == 1:
        return lambda b, i: (0, 0, 0)
    return lambda b, i: (b, 0, 0)


def _modulation_kernel(cond_ref, w_ref, b_ref, o_ref):
    a = _silu(cond_ref[...]).astype(BF16)
    o_ref[0] = _dot(a, w_ref[0].astype(BF16)) + b_ref[0]


def _modulation(cond, ada_w, ada_b):
    r, d = cond.shape
    depth, _, n = ada_w.shape
    tn = 512
    return pl.pallas_call(
        _modulation_kernel,
        grid=(depth, n // tn),
        in_specs=[pl.BlockSpec((r, d), lambda l, j: (0, 0)),
                  pl.BlockSpec((1, d, tn), lambda l, j: (l, 0, j)),
                  pl.BlockSpec((1, 1, tn), lambda l, j: (l, 0, j))],
        out_specs=pl.BlockSpec((1, r, tn), lambda l, j: (l, 0, j)),
        out_shape=jax.ShapeDtypeStruct((depth, r, n), F32),
        compiler_params=_params(2),
        name="adaln_modulation",
    )(cond, ada_w, ada_b.reshape(depth, 1, n))


def _halo_specs(tm, l, d):
    per = tm // HALO
    last = l // HALO - 1
    return [pl.BlockSpec((1, tm, d), lambda b, i: (b, i, 0)),
            pl.BlockSpec((1, HALO, d), lambda b, i: (b, jnp.maximum(i * per - 1, 0), 0)),
            pl.BlockSpec((1, HALO, d), lambda b, i: (b, jnp.minimum((i + 1) * per, last), 0))]


def _halo_rows(x_ref, xp_ref, xn_ref, nw, sc, sh):
    i = pl.program_id(1)
    has_prev = (i > 0).astype(F32)
    has_next = (i < pl.num_programs(1) - 1).astype(F32)
    hp = _norm_mod(xp_ref[0], nw, sc, sh) * has_prev
    hm = _norm_mod(x_ref[0], nw, sc, sh)
    hn = _norm_mod(xn_ref[0], nw, sc, sh) * has_next
    return jnp.concatenate([hp, hm, hn], axis=0).astype(BF16)


def _conv3(u_ref, tm, w, b):
    return (u_ref[pl.ds(HALO - 1, tm), :] * w[0:1] + u_ref[pl.ds(HALO, tm), :] * w[1:2]
            + u_ref[pl.ds(HALO + 1, tm), :] * w[2:3] + b)


def _conv3_rows(u, tm, w, b):
    rows = u.shape[0]
    y = pltpu.roll(u, 1, 0) * w[0:1] + u * w[1:2] + pltpu.roll(u, rows - 1, 0) * w[2:3] + b
    return y[HALO:HALO + tm]


def _ffn_kernel(x_ref, xp_ref, xn_ref, sc_ref, sh_ref, g_ref, nw_ref, wu_ref, cw_ref, cb_ref, wd_ref, o_ref,
                h_buf, a_buf, *, tm):
    f, fc = FFN_HIDDEN, FFN_COL_CHUNK
    h_buf[...] = _halo_rows(x_ref, xp_ref, xn_ref, nw_ref[...], sc_ref[0], sh_ref[0])
    for c in range(f // fc):
        gs = slice(c * fc, (c + 1) * fc)
        vs = slice(f + c * fc, f + (c + 1) * fc)
        gate = _conv3_rows(_dot(h_buf[...], wu_ref[:, gs]), tm, cw_ref[:, gs], cb_ref[:, gs])
        val = _conv3_rows(_dot(h_buf[...], wu_ref[:, vs]), tm, cw_ref[:, vs], cb_ref[:, vs])
        a_buf[:, gs] = (_silu(gate) * val).astype(BF16)
    for n in range(D_MODEL // fc):
        cs = slice(n * fc, (n + 1) * fc)
        o_ref[0, :, cs] = x_ref[0, :, cs] + g_ref[0, :, cs] * _dot(a_buf[...], wd_ref[:, cs])


def _conv_ffn_residual(x, sc, sh, g, nw, wts, tm):
    b, l, d = x.shape
    mod = pl.BlockSpec((1, 1, d), _mod_index(sc.shape[0]))
    once = lambda a: pl.BlockSpec(a.shape, lambda b_, i: (0, 0), pipeline_mode=pl.Buffered(1))
    return pl.pallas_call(
        functools.partial(_ffn_kernel, tm=tm),
        grid=(b, l // tm),
        in_specs=_halo_specs(tm, l, d) + [mod, mod, mod, pl.BlockSpec((1, d), lambda b_, i: (0, 0))]
        + [once(w) for w in wts],
        out_specs=pl.BlockSpec((1, tm, d), lambda b_, i: (b_, i, 0)),
        out_shape=jax.ShapeDtypeStruct((b, l, d), F32),
        scratch_shapes=[pltpu.VMEM((tm + 2 * HALO, d), BF16), pltpu.VMEM((tm, FFN_HIDDEN), BF16)],
        compiler_params=_params(2),
        name="conv_ffn",
    )(x, x, x, sc, sh, g, nw, *wts)


def _ffn_weights(w_up, conv_w, conv_b, w_down):
    return w_up.astype(BF16), conv_w, conv_b[None], w_down.astype(BF16)


def _out_proj_kernel(a_ref, x_ref, g_ref, w_ref, o_ref):
    o_ref[0] = x_ref[0] + g_ref[0] * _dot(a_ref[0].astype(BF16), w_ref[...])


def _out_proj_residual(a, x, g, w, tm):
    b, l, d = x.shape
    k = a.shape[-1]
    return pl.pallas_call(
        _out_proj_kernel,
        grid=(b, l // tm),
        in_specs=[pl.BlockSpec((1, tm, k), lambda b_, i: (b_, i, 0)),
                  pl.BlockSpec((1, tm, d), lambda b_, i: (b_, i, 0)),
                  pl.BlockSpec((1, 1, d), _mod_index(g.shape[0])),
                  pl.BlockSpec((k, d), lambda b_, i: (0, 0))],
        out_specs=pl.BlockSpec((1, tm, d), lambda b_, i: (b_, i, 0)),
        out_shape=jax.ShapeDtypeStruct((b, l, d), F32),
        compiler_params=_params(2),
        name="out_proj_residual",
    )(a, x, g, w)


GLA_W1_COLS = GLA_IN + LANES


def _gla_proj_kernel(x_ref, sc_ref, sh_ref, nw_ref, w1_ref, w2_ref, bg_ref, qk_ref, vr_ref, lg_ref):
    h = _norm_mod(x_ref[0], nw_ref[...], sc_ref[0], sh_ref[0]).astype(BF16)
    p = _dot(h, w1_ref[...])
    qk_ref[0] = p[:, :2 * GLA_DK]
    vr_ref[0] = p[:, 2 * GLA_DK:GLA_IN].astype(BF16)
    z = _dot(p[:, GLA_IN:].astype(BF16), w2_ref[...]) + bg_ref[...]
    lg_ref[0] = jax.nn.log_sigmoid(z) * (1.0 / GLA_TAU)


def _gla_project(x, sc, sh, nw, w1, w2, bg, tm):
    b, l, d = x.shape
    mod = pl.BlockSpec((1, 1, d), _mod_index(sc.shape[0]))
    const = lambda a: pl.BlockSpec(a.shape, lambda b_, i: (0, 0))
    return pl.pallas_call(
        _gla_proj_kernel,
        grid=(b, l // tm),
        in_specs=[pl.BlockSpec((1, tm, d), lambda b_, i: (b_, i, 0)), mod, mod, const(nw), const(w1), const(w2),
                  const(bg)],
        out_specs=[pl.BlockSpec((1, tm, 2 * GLA_DK), lambda b_, i: (b_, i, 0)),
                   pl.BlockSpec((1, tm, 2 * GLA_DV), lambda b_, i: (b_, i, 0)),
                   pl.BlockSpec((1, tm, 2 * GLA_DK), lambda b_, i: (b_, i, 0))],
        out_shape=[jax.ShapeDtypeStruct((b, l, 2 * GLA_DK), F32), jax.ShapeDtypeStruct((b, l, 2 * GLA_DV), BF16),
                   jax.ShapeDtypeStruct((b, l, 2 * GLA_DK), F32)],
        compiler_params=_params(2),
        name="gla_in_proj",
    )(x, sc, sh, nw, w1, w2, bg)


def _gla_tables(c):
    n_lev = int(np.log2(c))
    idx = np.arange(c)
    i, j = idx[:, None], idx[None, :]
    mask = np.zeros((2, n_lev + 1, c, c), np.float32)
    sign = np.zeros((2, n_lev, c, LANES), np.float32)
    step = np.zeros((2, n_lev, c, LANES), np.float32)
    for d in range(2):
        p = idx if d == 0 else c - 1 - idx
        pi, pj = p[:, None], p[None, :]
        mask[d, 0] = np.eye(c)
        for lev in range(n_lev):
            m = c >> (lev + 1)
            upper = (p % (2 * m)) >= m
            sign[d, lev] = np.where(upper, 1.0, -1.0)[:, None]
            mask[d, lev + 1] = (pi // (2 * m) == pj // (2 * m)) & ((pi % (2 * m)) >= m) & ((pj % (2 * m)) < m)
            step[d, lev] = (p >= (1 << lev)).astype(np.float32)[:, None]
    return jnp.asarray(sign), jnp.asarray(step), jnp.asarray(mask)


def _gla_exponents(lg, g_buf, sign_ref, step_ref, d, c, n_lev):
    back = d == 1
    g = lg
    for s in range(n_lev):
        sh = 1 << s
        g = g + step_ref[d, s] * pltpu.roll(g, (c - sh) if back else sh, 0)
    g_buf[...] = g
    total = g_buf[0:1, :] if back else g_buf[c - 1:c, :]
    row = lax.broadcasted_iota(jnp.int32, (c, LANES), 0)
    levels = []
    for lev in range(n_lev):
        m = c >> (lev + 1)
        if 2 * m >= 8:
            ref_rows = [jnp.broadcast_to(g_buf[ob * 2 * m + (m if back else m - 1):ob * 2 * m + (m if back else m - 1) + 1, :],
                                         (2 * m, LANES)) for ob in range(c // (2 * m))]
            big = jnp.concatenate(ref_rows, axis=0) if len(ref_rows) > 1 else ref_rows[0]
        else:
            tgt = m if back else m - 1
            pos = row % (2 * m)
            big = g
            for k in range(2 * m):
                if k != tgt:
                    big = jnp.where(pos == k, pltpu.roll(g, (k - tgt) % c, 0), big)
        levels.append(sign_ref[d, lev] * (g - big))
    return g, total - g, total, levels


def _gla_scan_kernel(qkf_ref, qkb_ref, vf_ref, vb_ref, lgf_ref, lgb_ref, sign_ref, step_ref, mask_ref, s0_ref,
                     of_ref, ob_ref, sT_ref, state, g_buf, *, c, n_lev):
    @pl.when(pl.program_id(1) == 0)
    def _():
        state[...] = s0_ref[:, 0]

    dirs = ((qkf_ref, vf_ref, lgf_ref, of_ref), (qkb_ref, vb_ref, lgb_ref, ob_ref))
    for d, (qk_ref, v_ref, lg_ref, o_ref) in enumerate(dirs):
        for h in range(GLA_HEADS):
            ks = slice(h * GLA_HK, (h + 1) * GLA_HK)
            kks = slice(GLA_DK + h * GLA_HK, GLA_DK + (h + 1) * GLA_HK)
            vs = slice(h * GLA_HV, (h + 1) * GLA_HV)
            q = qk_ref[0, :, ks] * (GLA_HK ** -0.5)
            k = qk_ref[0, :, kks]
            vb = v_ref[0, :, vs]
            g_in, g_out, g_tot, g_lev = _gla_exponents(lg_ref[0, :, ks], g_buf.at[d, h], sign_ref, step_ref, d, c, n_lev)
            att = mask_ref[d, 0] * _dot_nt(q.astype(BF16), k.astype(BF16))
            for lev in range(n_lev):
                e = jnp.exp(g_lev[lev])
                att = att + mask_ref[d, lev + 1] * _dot_nt((q * e).astype(BF16), (k * e).astype(BF16))
            st = state[d, h]
            o = _dot(att.astype(BF16), vb) + _dot_nt((q * jnp.exp(g_in)).astype(BF16), st.astype(BF16))
            o_ref[0, :, vs] = o.astype(o_ref.dtype)
            st = st * jnp.exp(g_tot) + _dot_tn(vb, (k * jnp.exp(g_out)).astype(BF16))
            state[d, h] = st
            sT_ref[d, 0, h] = st


def _gla_scan(qk, vr, lg, tables, s0):
    b, l, _ = qk.shape
    c = SCAN_CHUNK
    n = l // c
    sign, step, mask = tables
    n_lev = mask.shape[1] - 1
    kern = functools.partial(_gla_scan_kernel, c=c, n_lev=n_lev)
    whole = lambda a: pl.BlockSpec(a.shape, lambda b_, i: (0,) * a.ndim)
    fwd = lambda blk: (lambda b_, i: (b_, i, blk))
    bwd = lambda blk: (lambda b_, i: (b_, n - 1 - i, blk))
    st_spec = pl.BlockSpec((2, 1, GLA_HEADS, GLA_HV, GLA_HK), lambda b_, i: (0, b_, 0, 0, 0))
    return pl.pallas_call(
        kern,
        grid=(b, n),
        in_specs=[pl.BlockSpec((1, c, 2 * GLA_DK), fwd(0)), pl.BlockSpec((1, c, 2 * GLA_DK), bwd(0)),
                  pl.BlockSpec((1, c, GLA_DV), fwd(0)), pl.BlockSpec((1, c, GLA_DV), bwd(0)),
                  pl.BlockSpec((1, c, GLA_DK), fwd(0)), pl.BlockSpec((1, c, GLA_DK), bwd(1)),
                  whole(sign), whole(step), whole(mask), st_spec],
        out_specs=[pl.BlockSpec((1, c, GLA_DV), fwd(0)), pl.BlockSpec((1, c, GLA_DV), bwd(0)), st_spec],
        out_shape=[jax.ShapeDtypeStruct((b, l, GLA_DV), BF16), jax.ShapeDtypeStruct((b, l, GLA_DV), BF16),
                   jax.ShapeDtypeStruct((2, b, GLA_HEADS, GLA_HV, GLA_HK), F32)],
        scratch_shapes=[pltpu.VMEM((2, GLA_HEADS, GLA_HV, GLA_HK), F32),
                        pltpu.VMEM((2, GLA_HEADS, c, GLA_HK), F32)],
        compiler_params=_params(2),
        name="gla_scan",
    )(qk, qk, vr, vr, lg, lg, sign, step, mask, s0)


def _gla_finish_kernel(of_ref, ob_ref, r_ref, x_ref, g_ref, nw_ref, w_ref, o_ref):
    o = of_ref[0].astype(F32) + ob_ref[0].astype(F32)
    heads = []
    for h in range(GLA_HEADS):
        oh = o[:, h * GLA_HV:(h + 1) * GLA_HV]
        ms = jnp.mean(oh * oh, axis=-1, keepdims=True)
        heads.append(oh * lax.rsqrt(ms + EPS) * nw_ref[...])
    y = jnp.concatenate(heads, axis=-1) * _silu(r_ref[0].astype(F32))
    o_ref[0] = x_ref[0] + g_ref[0] * _dot(y.astype(BF16), w_ref[...])


def _gla_finish(o_f, o_b, vr, x, g, nw, w, tm):
    b, l, d = x.shape
    return pl.pallas_call(
        _gla_finish_kernel,
        grid=(b, l // tm),
        in_specs=[pl.BlockSpec((1, tm, GLA_DV), lambda b_, i: (b_, i, 0)),
                  pl.BlockSpec((1, tm, GLA_DV), lambda b_, i: (b_, i, 0)),
                  pl.BlockSpec((1, tm, GLA_DV), lambda b_, i: (b_, i, 1)),
                  pl.BlockSpec((1, tm, d), lambda b_, i: (b_, i, 0)),
                  pl.BlockSpec((1, 1, d), _mod_index(g.shape[0])),
                  pl.BlockSpec((1, GLA_HV), lambda b_, i: (0, 0)),
                  pl.BlockSpec((GLA_DV, d), lambda b_, i: (0, 0))],
        out_specs=pl.BlockSpec((1, tm, d), lambda b_, i: (b_, i, 0)),
        out_shape=jax.ShapeDtypeStruct((b, l, d), F32),
        compiler_params=_params(2),
        name="gla_finish",
    )(o_f, o_b, vr, x, g, nw, w)


def _gla_weights(w_in, w_g1, w_g2, b_g, norm_w, w_out):
    d = w_in.shape[0]
    r = GLA_GATE_RANK
    w1 = jnp.concatenate([w_in, w_g1[0], w_g1[1], jnp.zeros((d, LANES - 2 * r), F32)], axis=1).astype(BF16)
    w2 = jnp.zeros((LANES, 2 * GLA_DK), F32)
    w2 = w2.at[0:r, :GLA_DK].set(w_g2[0]).at[r:2 * r, GLA_DK:].set(w_g2[1]).astype(BF16)
    return w1, w2, b_g.reshape(1, 2 * GLA_DK), norm_w.reshape(1, GLA_HV), w_out.astype(BF16)


def _gla_mixer(xc, xl, mod_c, mod_l, nw, wts, tables, ctx_out, tm_l, tm_c):
    w1, w2, bg, norm_w, w_out = wts
    (csc, csh, cg), (sc, sh, g) = mod_c, mod_l
    qkc, vrc, lgc = _gla_project(xc, csc, csh, nw, w1, w2, bg, tm_c)
    qkl, vrl, lgl = _gla_project(xl, sc, sh, nw, w1, w2, bg, tm_l)
    zero = jnp.zeros((2, xc.shape[0], GLA_HEADS, GLA_HV, GLA_HK), F32)
    ocf, ocb, s_ctx = _gla_scan(qkc, vrc, lgc, tables, zero)
    olf, olb, _ = _gla_scan(qkl, vrl, lgl, tables, s_ctx)
    xl = _gla_finish(olf, olb, vrl, xl, g, norm_w, w_out, tm_l)
    if ctx_out:
        xc = _gla_finish(ocf, ocb, vrc, xc, cg, norm_w, w_out, tm_c)
    return xc, xl


MLA_HEAD_PAD = 2 * LANES
MLA_V_PAD = 2 * LANES


def _rot_cols(w):
    half = MLA_ROPE // 2
    return jnp.concatenate([-w[..., half:], w[..., :half]], axis=-1)


def _swap_halves(w):
    half = MLA_ROPE // 2
    return jnp.concatenate([w[..., half:], w[..., :half]], axis=-1)


def _mla_proj_kernel(x_ref, sc_ref, sh_ref, nw_ref, win_ref, qan_ref, kvan_ref, wuq_ref, wukv_ref,
                     qna_ref, qnb_ref, kna_ref, knb_ref, cs_ref, q_ref, k_ref, v_ref):
    h = _norm_mod(x_ref[0], nw_ref[...], sc_ref[0], sh_ref[0]).astype(BF16)
    p = _dot(h, win_ref[...])
    cq = p[:, :MLA_Q_RANK]
    ckv = p[:, MLA_Q_RANK:MLA_Q_RANK + MLA_KV_RANK]
    kpe = p[:, MLA_Q_RANK + MLA_KV_RANK:]
    cq = cq * lax.rsqrt(jnp.mean(cq * cq, axis=-1, keepdims=True) + EPS) * qan_ref[...]
    ckv = ckv * lax.rsqrt(jnp.mean(ckv * ckv, axis=-1, keepdims=True) + EPS) * kvan_ref[...]
    qall = _dot(cq.astype(BF16), wuq_ref[...])
    kvall = _dot(ckv.astype(BF16), wukv_ref[...])
    cs = cs_ref[...]
    lane = lax.broadcasted_iota(jnp.int32, (1, LANES), 1)
    keep = (lane < MLA_ROPE).astype(F32)
    kpe_ss = 0.5 * jnp.sum(kpe * kpe, axis=-1, keepdims=True)
    scale = MLA_QK ** -0.5 * np.log2(np.e)

    def head(nope, pe, na, nb, mult):
        ss = jnp.sum(nope * nope, axis=-1, keepdims=True) + 0.5 * jnp.sum(pe * pe, axis=-1, keepdims=True)
        s = lax.rsqrt(ss * (1.0 / MLA_QK) + EPS)
        t = pe * s * nb * cs
        rope = (t + pltpu.roll(t, MLA_ROPE, 1)) * keep
        return jnp.concatenate([nope * s * na * mult, rope * mult], axis=-1).astype(BF16)

    for hd in range(MLA_HEADS):
        qh = qall[:, hd * MLA_HEAD_PAD:(hd + 1) * MLA_HEAD_PAD]
        q_ref[0, hd] = head(qh[:, :LANES], qh[:, LANES:], qna_ref[...], qnb_ref[...], scale)
        kh = kvall[:, hd * 2 * LANES:hd * 2 * LANES + LANES]
        ss = jnp.sum(kh * kh, axis=-1, keepdims=True) + kpe_ss
        s = lax.rsqrt(ss * (1.0 / MLA_QK) + EPS)
        t = kpe * s * knb_ref[...] * cs
        rope = (t + pltpu.roll(t, MLA_ROPE, 1)) * keep
        k_ref[0, hd] = jnp.concatenate([kh * s * kna_ref[...], rope], axis=-1).astype(BF16)
        vh = kvall[:, hd * 2 * LANES + LANES:(hd + 1) * 2 * LANES]
        v_ref[0, hd] = jnp.concatenate([vh, jnp.ones_like(vh)], axis=-1).astype(BF16)


def _mla_project(x, sc, sh, nw, wts, cs, tm):
    b, l, d = x.shape
    win, qan, kvan, wuq, wukv, qna, qnb, kna, knb = wts[:9]
    mod = pl.BlockSpec((1, 1, d), _mod_index(sc.shape[0]))
    const = lambda a: pl.BlockSpec(a.shape, lambda b_, i: (0, 0))
    cs_spec = (pl.BlockSpec((tm, LANES), lambda b_, i: (i, 0)) if cs.shape[0] == l
               else pl.BlockSpec((tm, LANES), lambda b_, i: (0, 0)))
    hspec = lambda w: pl.BlockSpec((1, MLA_HEADS, tm, w), lambda b_, i: (b_, 0, i, 0))
    return pl.pallas_call(
        _mla_proj_kernel,
        grid=(b, l // tm),
        in_specs=[pl.BlockSpec((1, tm, d), lambda b_, i: (b_, i, 0)), mod, mod, const(nw), const(win), const(qan),
                  const(kvan), const(wuq), const(wukv), const(qna), const(qnb), const(kna), const(knb), cs_spec],
        out_specs=[hspec(MLA_HEAD_PAD), hspec(MLA_HEAD_PAD), hspec(MLA_V_PAD)],
        out_shape=[jax.ShapeDtypeStruct((b, MLA_HEADS, l, MLA_HEAD_PAD), BF16),
                   jax.ShapeDtypeStruct((b, MLA_HEADS, l, MLA_HEAD_PAD), BF16),
                   jax.ShapeDtypeStruct((b, MLA_HEADS, l, MLA_V_PAD), BF16)],
        compiler_params=_params(2),
        name="mla_project",
    )(x, sc, sh, nw, win, qan, kvan, wuq, wukv, qna, qnb, kna, knb, cs)


def _flash_kernel(*refs, tk, n_kv, has_ctx):
    if has_ctx:
        q_ref, k_ref, v_ref, kc_ref, vc_ref, o_ref = refs
    else:
        q_ref, k_ref, v_ref, o_ref = refs
    q = q_ref[0, 0]

    def update(carry, kj, vj):
        m, acc = carry
        s = _dot_nt(q, kj)
        m_new = jnp.maximum(m, jnp.max(s, axis=-1, keepdims=True))
        p = jnp.exp2(s - m_new)
        return m_new, jnp.exp2(m - m_new) * acc + _dot(p.astype(BF16), vj)

    tq = q.shape[0]
    carry = (jnp.full((tq, 1), -jnp.inf, F32), jnp.zeros((tq, MLA_V_PAD), F32))
    if has_ctx:
        carry = update(carry, kc_ref[0, 0], vc_ref[0, 0])
    for j in range(n_kv):
        carry = update(carry, k_ref[0, 0, j * tk:(j + 1) * tk, :], v_ref[0, 0, j * tk:(j + 1) * tk, :])
    acc = carry[1]
    o_ref[0] = (acc[:, :MLA_V] / acc[:, MLA_V:]).astype(o_ref.dtype)


def _flash_attention(q, k, v, kc, vc, tq, tk):
    b, hds, l, dq = q.shape
    lk = k.shape[2]
    has_ctx = kc is not None
    kern = functools.partial(_flash_kernel, tk=tk, n_kv=lk // tk, has_ctx=has_ctx)
    whole = lambda a: pl.BlockSpec((1, 1) + a.shape[2:], lambda b_, h, i: (b_, h, 0, 0))
    in_specs = [pl.BlockSpec((1, 1, tq, dq), lambda b_, h, i: (b_, h, i, 0)), whole(k), whole(v)]
    args = [q, k, v]
    if has_ctx:
        in_specs += [whole(kc), whole(vc)]
        args += [kc, vc]
    return pl.pallas_call(
        kern,
        grid=(b, hds, l // tq),
        in_specs=in_specs,
        out_specs=pl.BlockSpec((1, tq, MLA_V), lambda b_, h, i: (b_, i, h)),
        out_shape=jax.ShapeDtypeStruct((b, l, hds * MLA_V), BF16),
        compiler_params=_params(3),
        name="mla_flash_attention",
    )(*args)


def _mla_weights(w_in, q_a_norm, w_uq, kv_a_norm, w_ukv, q_norm, k_norm, w_out):
    kpe = w_in[:, MLA_Q_RANK + MLA_KV_RANK:]
    win = jnp.concatenate([w_in, _rot_cols(kpe)], axis=1).astype(BF16)
    wq = w_uq.reshape(MLA_Q_RANK, MLA_HEADS, MLA_QK)
    wq = jnp.concatenate([wq, _rot_cols(wq[..., MLA_NOPE:])], axis=-1)
    wuq = wq.reshape(MLA_Q_RANK, MLA_HEADS * MLA_HEAD_PAD).astype(BF16)
    gains = lambda nrm: (nrm[None, :MLA_NOPE],
                         jnp.concatenate([nrm[MLA_NOPE:], _swap_halves(nrm[MLA_NOPE:])])[None])
    qna, qnb = gains(q_norm)
    kna, knb = gains(k_norm)
    return (win, q_a_norm[None], kv_a_norm[None], wuq, w_ukv.astype(BF16), qna, qnb, kna, knb, w_out.astype(BF16))


def _rope_table(l):
    rows = l // GRID_W
    pos_r = jnp.repeat(jnp.arange(rows, dtype=F32), GRID_W)
    pos_c = jnp.tile(jnp.arange(GRID_W, dtype=F32), rows)
    n_freq = MLA_ROPE // 4
    inv = ROPE_THETA ** (-jnp.arange(n_freq, dtype=F32) / n_freq)
    ang = jnp.concatenate([pos_r[:, None] * inv, pos_c[:, None] * inv], axis=-1)
    return jnp.concatenate([jnp.cos(ang), jnp.cos(ang), jnp.sin(ang), jnp.sin(ang)], axis=-1)


def _mla_mixer(xc, xl, mod_c, mod_l, nw, wts, ctx_out, tm_l, tm_c):
    (csc, csh, cg), (sc, sh, g) = mod_c, mod_l
    w_out = wts[9]
    lc, l = xc.shape[1], xl.shape[1]
    no_rope = jnp.concatenate([jnp.ones((tm_c, LANES // 2), F32), jnp.zeros((tm_c, LANES // 2), F32)], axis=-1)
    qc, kc, vc = _mla_project(xc, csc, csh, nw, wts, no_rope, tm_c)
    ql, kl, vl = _mla_project(xl, sc, sh, nw, wts, _rope_table(l), tm_l)
    tq = min(1024, l)
    al = _flash_attention(ql, kl, vl, kc, vc, tq, min(512, l))
    xl = _out_proj_residual(al, xl, g, w_out, tm_l)
    if ctx_out:
        ac = _flash_attention(qc, kc, vc, None, None, lc, lc)
        xc = _out_proj_residual(ac, xc, cg, w_out, tm_c)
    return xc, xl


def _ssd_proj_kernel(x_ref, xp_ref, xn_ref, sc_ref, sh_ref, nw_ref, wz_ref, wx_ref, wdt_ref, cw_ref, cb_ref,
                     dtb_ref, z_ref, xbc_ref, dt_ref, h_buf, *, tm):
    cc = SSD_COL_CHUNK
    h_buf[...] = _halo_rows(x_ref, xp_ref, xn_ref, nw_ref[...], sc_ref[0], sh_ref[0])
    hm = h_buf[pl.ds(HALO, tm), :]
    z_ref[0] = _dot(hm, wz_ref[...]).astype(z_ref.dtype)
    dt = jax.nn.softplus(_dot(hm, wdt_ref[...]) + dtb_ref[...])
    dt_ref[0, 0] = dt[:, :LANES]
    dt_ref[1, 0] = dt[:, LANES:]
    for c in range(SSM_XBC // cc):
        cs = slice(c * cc, (c + 1) * cc)
        u = _conv3_rows(_dot(h_buf[...], wx_ref[:, cs]), tm, cw_ref[:, cs], cb_ref[:, cs])
        xbc_ref[0, :, cs] = _silu(u).astype(xbc_ref.dtype)


def _ssd_project(x, sc, sh, nw, wts, tm):
    b, l, d = x.shape
    wz, wx, wdt, cw, cb, dtb = wts[:6]
    mod = pl.BlockSpec((1, 1, d), _mod_index(sc.shape[0]))
    const2 = lambda a: pl.BlockSpec(a.shape, lambda b_, i: (0, 0), pipeline_mode=pl.Buffered(1))
    kern = functools.partial(_ssd_proj_kernel, tm=tm)
    return pl.pallas_call(
        kern,
        grid=(b, l // tm),
        in_specs=_halo_specs(tm, l, d) + [mod, mod, const2(nw), const2(wz), const2(wx), const2(wdt), const2(cw),
                                           const2(cb), const2(dtb)],
        out_specs=[pl.BlockSpec((1, tm, SSM_INNER), lambda b_, i: (b_, i, 0)),
                   pl.BlockSpec((1, tm, SSM_XBC), lambda b_, i: (b_, i, 0)),
                   pl.BlockSpec((2, 1, tm, LANES), lambda b_, i: (0, b_, i, 0))],
        out_shape=[jax.ShapeDtypeStruct((b, l, SSM_INNER), BF16),
                   jax.ShapeDtypeStruct((b, l, SSM_XBC), BF16),
                   jax.ShapeDtypeStruct((2, b, l, LANES), F32)],
        scratch_shapes=[pltpu.VMEM((tm + 2 * HALO, d), BF16)],
        compiler_params=_params(2),
        name="ssd_in_proj_conv",
    )(x, x, x, sc, sh, nw, wz, wx, wdt, cw, cb, dtb)


def _ssd_tables(c):
    idx = np.arange(c)
    i, t = idx[:, None], idx[None, :]
    ones = np.ones((8, c), np.float32)
    fwd = np.concatenate([(t <= i).astype(np.float32), ones], 0)
    bwd = np.concatenate([(t >= i).astype(np.float32), ones], 0)
    mask = np.stack([(t <= i), (t >= i)]).astype(np.float32)
    h = np.arange(SSM_HEADS)
    e_seg = np.zeros((LANES, SSM_HEADS, c), np.float32)
    e_seg[h, h, :] = 1.0
    e_seg[SSM_HEADS + h, h, :] = 1.0
    e_x = np.zeros((LANES, 2, SSM_HEADS, SSM_HEADDIM), np.float32)
    for q in range(2):
        e_x[2 * q * SSM_HEADS + h, q, h, :] = 1.0
        e_x[(2 * q + 1) * SSM_HEADS + h, q, h, :] = 1.0
    return (jnp.asarray(np.stack([fwd, bwd]), BF16), jnp.asarray(mask, F32),
            jnp.asarray(e_seg.reshape(LANES, SSM_HEADS * c), BF16),
            jnp.asarray(e_x.reshape(LANES, 2 * SSM_INNER), BF16))


def _pack_hi_lo(parts):
    out = None
    for n, v in enumerate(parts):
        hi = v.astype(BF16).astype(F32)
        for j, piece in enumerate((hi, v - hi)):
            shift = (2 * n + j) * SSM_HEADS
            piece = piece if shift == 0 else pltpu.roll(piece, shift, 1)
            out = piece if out is None else out + piece
    return out.astype(BF16)


def _ssd_scan_kernel(xf_ref, bf_ref, cf_ref, dtf_ref, xb_ref, bb_ref, cb_ref, dtb_ref, a_ref, tri_ref, mask_ref,
                     eseg_ref, ex_ref, s0_ref, yf_ref, yb_ref, sf_ref, state, *, c):
    @pl.when(pl.program_id(1) == 0)
    def _():
        state[...] = s0_ref[:, 0]

    lane = lax.broadcasted_iota(jnp.int32, (1, LANES), 1)
    valid = (lane < SSM_HEADS).astype(F32)
    left = lane < SSM_HEADDIM
    per_group = SSM_HEADS // SSM_GROUPS
    gw = per_group * SSM_HEADDIM
    dirs = ((xf_ref, bf_ref, cf_ref, dtf_ref, yf_ref), (xb_ref, bb_ref, cb_ref, dtb_ref, yb_ref))
    for d, (x_ref, b_ref, c_ref, dt_ref, y_ref) in enumerate(dirs):
        dt = dt_ref[0, 0] * valid
        cums = _split_dot(tri_ref[d], dt * a_ref[d])
        cum = cums[:c]
        total = cums[c:c + 1]
        e_in = jnp.exp(cum) * valid
        w_out = jnp.exp(total - cum) * dt
        e_tot = jnp.exp(total)
        cum_col = _dot(_pack_hi_lo([cum]), eseg_ref[...])
        xs = _dot(_pack_hi_lo([e_in, w_out]), ex_ref[...])
        cum_t = jnp.transpose(cum)
        dt_t = jnp.transpose(dt)
        causal = mask_ref[d] > 0.5
        for g in range(SSM_GROUPS):
            bm = b_ref[0, :, g * SSM_STATE:(g + 1) * SSM_STATE]
            cm = c_ref[0, :, g * SSM_STATE:(g + 1) * SSM_STATE]
            scores = _dot_nt(cm, bm)
            sg = state[d, g * gw:(g + 1) * gw, :]
            y_in = _dot_nt(cm, sg.astype(BF16))
            x_g = x_ref[0, :, g * gw:(g + 1) * gw]
            for pr in range(per_group // 2):
                h0 = g * per_group + 2 * pr
                col = g * gw + pr * LANES
                ms = []
                for h in (h0, h0 + 1):
                    seg = jnp.exp(cum_col[:, h * c:(h + 1) * c] - cum_t[h:h + 1, :])
                    ms.append((scores * jnp.where(causal, seg, 0.0) * dt_t[h:h + 1, :]).astype(BF16))
                xp = x_g[:, pr * LANES:(pr + 1) * LANES]
                zero = jnp.zeros_like(xp)
                xbd = jnp.concatenate([jnp.where(left, xp, zero), jnp.where(left, zero, xp)], axis=0)
                y = _dot(jnp.concatenate(ms, axis=1), xbd)
                y = y + xs[:, col:col + LANES] * y_in[:, pr * LANES:(pr + 1) * LANES]
                y_ref[0, :, col:col + LANES] = y.astype(y_ref.dtype)
            wx = (x_g.astype(F32) * xs[:, SSM_INNER + g * gw:SSM_INNER + (g + 1) * gw]).astype(BF16)
            upd = _dot_tn(wx, bm)
            dec = jnp.concatenate(
                [jnp.broadcast_to(e_tot[:, g * per_group + j:g * per_group + j + 1], (SSM_HEADDIM, SSM_STATE))
                 for j in range(per_group)], axis=0)
            state[d, g * gw:(g + 1) * gw, :] = sg * dec + upd
    sf_ref[:, 0] = state[...]


def _ssd_scan(xbc, dt2, a_rows, tables, s0):
    b, l, _ = xbc.shape
    c = SCAN_CHUNK
    n = l // c
    tri, mask, e_seg, e_x = tables
    xblk = SSM_INNER // SSM_BC
    kern = functools.partial(_ssd_scan_kernel, c=c)
    fwd = lambda blk: (lambda b_, i: (b_, i, blk))
    bwd = lambda blk: (lambda b_, i: (b_, n - 1 - i, blk))
    stream = lambda at: [pl.BlockSpec((1, c, SSM_INNER), at(0)), pl.BlockSpec((1, c, SSM_BC), at(xblk)),
                         pl.BlockSpec((1, c, SSM_BC), at(xblk + 1))]
    whole = lambda a: pl.BlockSpec(a.shape, lambda b_, i: (0,) * a.ndim)
    st_spec = pl.BlockSpec((2, 1, SSM_INNER, SSM_STATE), lambda b_, i: (0, b_, 0, 0))
    return pl.pallas_call(
        kern,
        grid=(b, n),
        in_specs=(stream(fwd) + [pl.BlockSpec((1, 1, c, LANES), lambda b_, i: (0, b_, i, 0))]
                  + stream(bwd) + [pl.BlockSpec((1, 1, c, LANES), lambda b_, i: (1, b_, n - 1 - i, 0))]
                  + [whole(a_rows), whole(tri), whole(mask), whole(e_seg), whole(e_x), st_spec]),
        out_specs=[pl.BlockSpec((1, c, SSM_INNER), fwd(0)), pl.BlockSpec((1, c, SSM_INNER), bwd(0)), st_spec],
        out_shape=[jax.ShapeDtypeStruct((b, l, SSM_INNER), BF16), jax.ShapeDtypeStruct((b, l, SSM_INNER), BF16),
                   jax.ShapeDtypeStruct((2, b, SSM_INNER, SSM_STATE), F32)],
        scratch_shapes=[pltpu.VMEM((2, SSM_INNER, SSM_STATE), F32)],
        compiler_params=_params(2),
        name="ssd_scan",
    )(xbc, xbc, xbc, dt2, xbc, xbc, xbc, dt2, a_rows, tri, mask, e_seg, e_x, s0)


def _ssd_finish_kernel(yf_ref, yb_ref, xs_ref, z_ref, x_ref, g_ref, dsk_ref, nw_ref, w_ref, o_ref):
    y = yf_ref[0].astype(F32) + yb_ref[0].astype(F32) + xs_ref[0].astype(F32) * dsk_ref[...]
    y = y * _silu(z_ref[0].astype(F32))
    gw = SSM_INNER // SSM_GROUPS
    parts = []
    for gi in range(SSM_GROUPS):
        yg = y[:, gi * gw:(gi + 1) * gw]
        ms = jnp.mean(yg * yg, axis=-1, keepdims=True)
        parts.append(yg * lax.rsqrt(ms + EPS) * nw_ref[:, gi * gw:(gi + 1) * gw])
    yn = jnp.concatenate(parts, axis=-1)
    o_ref[0] = x_ref[0] + g_ref[0] * _dot(yn.astype(BF16), w_ref[...])


def _ssd_finish(y_f, y_b, xbc, z, x, g, dsk, nw, w, tm):
    b, l, d = x.shape
    return pl.pallas_call(
        _ssd_finish_kernel,
        grid=(b, l // tm),
        in_specs=[pl.BlockSpec((1, tm, SSM_INNER), lambda b_, i: (b_, i, 0)),
                  pl.BlockSpec((1, tm, SSM_INNER), lambda b_, i: (b_, i, 0)),
                  pl.BlockSpec((1, tm, SSM_INNER), lambda b_, i: (b_, i, 0)),
                  pl.BlockSpec((1, tm, SSM_INNER), lambda b_, i: (b_, i, 0)),
                  pl.BlockSpec((1, tm, d), lambda b_, i: (b_, i, 0)),
                  pl.BlockSpec((1, 1, d), _mod_index(g.shape[0])),
                  pl.BlockSpec((1, SSM_INNER), lambda b_, i: (0, 0)),
                  pl.BlockSpec((1, SSM_INNER), lambda b_, i: (0, 0)),
                  pl.BlockSpec((SSM_INNER, d), lambda b_, i: (0, 0))],
        out_specs=pl.BlockSpec((1, tm, d), lambda b_, i: (b_, i, 0)),
        out_shape=jax.ShapeDtypeStruct((b, l, d), F32),
        compiler_params=_params(2),
        name="ssd_finish",
    )(y_f, y_b, xbc, z, x, g, dsk, nw, w)


def _ssd_weights(w_in, conv_w, conv_b, dt_bias, a_log, d_skip, norm_w, w_out):
    d = w_in.shape[0]
    wz = w_in[:, :SSM_INNER].astype(BF16)
    wx = w_in[:, SSM_INNER:SSM_INNER + SSM_XBC].astype(BF16)
    wdt_raw = w_in[:, SSM_INNER + SSM_XBC:]
    pad = jnp.zeros((d, LANES - SSM_HEADS), F32)
    wdt = jnp.concatenate([wdt_raw[:, :SSM_HEADS], pad, wdt_raw[:, SSM_HEADS:], pad], axis=1).astype(BF16)
    bpad = jnp.zeros((LANES - SSM_HEADS,), F32)
    dtb = jnp.concatenate([dt_bias[0], bpad, dt_bias[1], bpad])[None]
    cw, cb = conv_w, conv_b[None]
    a = -jnp.exp(a_log.astype(F32))
    a_rows = jnp.concatenate([a, jnp.zeros((2, LANES - SSM_HEADS), F32)], axis=1).reshape(2, 1, LANES)
    dsk = jnp.repeat(d_skip, SSM_HEADDIM)[None]
    return wz, wx, wdt, cw, cb, dtb, a_rows, dsk, norm_w[None], w_out.astype(BF16)


def _ssd_mixer(xc, xl, mod_c, mod_l, nw, wts, tables, ctx_out, tm_l, tm_c):
    a_rows, dsk, norm_w, w_out = wts[6:]
    (csc, csh, cg), (sc, sh, g) = mod_c, mod_l
    zc, xbc_c, dtc = _ssd_project(xc, csc, csh, nw, wts, tm_c)
    zl, xbc_l, dtl = _ssd_project(xl, sc, sh, nw, wts, tm_l)
    zero = jnp.zeros((2, xc.shape[0], SSM_INNER, SSM_STATE), F32)
    ycf, ycb, s_ctx = _ssd_scan(xbc_c, dtc, a_rows, tables, zero)
    ylf, ylb, _ = _ssd_scan(xbc_l, dtl, a_rows, tables, s_ctx)
    xl = _ssd_finish(ylf, ylb, xbc_l, zl, xl, g, dsk, norm_w, w_out, tm_l)
    if ctx_out:
        xc = _ssd_finish(ycf, ycb, xbc_c, zc, xc, cg, dsk, norm_w, w_out, tm_c)
    return xc, xl


def kernel(x, c, ctx, c_ctx, ada_w, ada_b, norm_mix_w, norm_ffn_w, ffn_w_up, ffn_conv_w, ffn_conv_b, ffn_w_down, gla_w_in, gla_w_g1, gla_w_g2, gla_b_g, gla_norm_w, gla_w_out, mla_w_in, mla_q_a_norm, mla_w_uq, mla_kv_a_norm, mla_w_ukv, mla_q_norm, mla_k_norm, mla_w_out, ssm_w_in, ssm_conv_w, ssm_conv_b, ssm_dt_bias, ssm_a_log, ssm_d, ssm_norm_w, ssm_w_out):
    bsz, l, d = x.shape
    lc = ctx.shape[1]
    tm_l = min(512, l)
    tm_c = lc
    cond = jnp.concatenate([c, c_ctx[None], jnp.zeros((8 - bsz - 1, d), F32)], axis=0)
    mods = _modulation(cond, ada_w, ada_b)
    gla_tables = _gla_tables(SCAN_CHUNK)
    ssd_tables = _ssd_tables(SCAN_CHUNK)
    xl, xc = x, ctx
    for i in range(DEPTH):
        kind, j = i % N_MIXERS, i // N_MIXERS
        ctx_out = i < DEPTH - 1
        m6 = [mods[i, :, k * d:(k + 1) * d] for k in range(N_MOD)]
        lat = [m[:bsz, None, :] for m in m6]
        cx = [m[bsz:bsz + 1, None, :] for m in m6]
        nw = norm_mix_w[i][None]
        mod_l, mod_c = (lat[1], lat[0], lat[2]), (cx[1], cx[0], cx[2])
        if kind == 0:
            wts = _gla_weights(gla_w_in[j], gla_w_g1[j], gla_w_g2[j], gla_b_g[j], gla_norm_w[j], gla_w_out[j])
            xc, xl = _gla_mixer(xc, xl, mod_c, mod_l, nw, wts, gla_tables, ctx_out, tm_l, tm_c)
        elif kind == 1:
            wts = _mla_weights(mla_w_in[j], mla_q_a_norm[j], mla_w_uq[j], mla_kv_a_norm[j], mla_w_ukv[j],
                               mla_q_norm[j], mla_k_norm[j], mla_w_out[j])
            xc, xl = _mla_mixer(xc, xl, mod_c, mod_l, nw, wts, ctx_out, tm_l, tm_c)
        else:
            wts = _ssd_weights(ssm_w_in[j], ssm_conv_w[j], ssm_conv_b[j], ssm_dt_bias[j], ssm_a_log[j], ssm_d[j],
                               ssm_norm_w[j], ssm_w_out[j])
            xc, xl = _ssd_mixer(xc, xl, mod_c, mod_l, nw, wts, ssd_tables, ctx_out, tm_l, tm_c)
        fw = _ffn_weights(ffn_w_up[i], ffn_conv_w[i], ffn_conv_b[i], ffn_w_down[i])
        nfw = norm_ffn_w[i][None]
        xl = _conv_ffn_residual(xl, lat[4], lat[3], lat[5], nfw, fw, min(FFN_ROWS, l))
        if ctx_out:
            xc = _conv_ffn_residual(xc, cx[4], cx[3], cx[5], nfw, fw, tm_c)
    return xl
```

```python
import functools

import numpy as np
import jax
import jax.numpy as jnp
from jax import lax
from jax.experimental import pallas as pl
from jax.experimental.pallas import tpu as pltpu

F32 = jnp.float32
BF16 = jnp.bfloat16

D_MODEL = 1024
DEPTH = 4
GRID_W = 64
N_MIXERS = 3
N_MOD = 6
EPS = 1e-6

GLA_HEADS = 4
GLA_DK = D_MODEL // 2
GLA_DV = D_MODEL
GLA_HK = GLA_DK // GLA_HEADS
GLA_HV = GLA_DV // GLA_HEADS
GLA_GATE_RANK = 16
GLA_TAU = 16.0
GLA_IN = 2 * GLA_DK + 2 * GLA_DV

MLA_HEADS = 8
MLA_NOPE = 128
MLA_ROPE = 64
MLA_QK = MLA_NOPE + MLA_ROPE
MLA_V = 128
MLA_Q_RANK = 256
MLA_KV_RANK = 128
ROPE_THETA = 10000.0

SSM_INNER = 2 * D_MODEL
SSM_HEADDIM = 64
SSM_HEADS = SSM_INNER // SSM_HEADDIM
SSM_STATE = 128
SSM_GROUPS = 4
SSM_BC = SSM_GROUPS * SSM_STATE
SSM_XBC = SSM_INNER + 2 * SSM_BC

FFN_HIDDEN = 2816

LANES = 128
HALO = 8
VMEM_LIMIT_BYTES = 56 * 1024 * 1024
SCAN_CHUNK = 128
FFN_COL_CHUNK = 256
FFN_ROWS = 1024
SSD_COL_CHUNK = 512


def _params(n_axes):
    return pltpu.CompilerParams(dimension_semantics=("arbitrary",) * n_axes,
                                vmem_limit_bytes=VMEM_LIMIT_BYTES)


def _dot(a, b):
    return jnp.dot(a, b, preferred_element_type=F32)


def _dot_nt(a, b):
    return lax.dot_general(a, b, (((1,), (1,)), ((), ())), preferred_element_type=F32)


def _dot_tn(a, b):
    return lax.dot_general(a, b, (((0,), (0,)), ((), ())), preferred_element_type=F32)


def _split_dot(mat_bf16, v_f32):
    hi = v_f32.astype(BF16)
    lo = (v_f32 - hi.astype(F32)).astype(BF16)
    return _dot(mat_bf16, hi) + _dot(mat_bf16, lo)


def _norm_mod(x, nw, sc, sh):
    ms = jnp.mean(x * x, axis=-1, keepdims=True)
    return (x * lax.rsqrt(ms + EPS) * nw) * (1.0 + sc) + sh


def _silu(x):
    return x * jax.nn.sigmoid(x)


def _mod_index(n_mod):
    if n_mod == 1:
        return lambda b, i: (0, 0, 0)
    return lambda b, i: (b, 0, 0)


def _modulation_kernel(cond_ref, w_ref, b_ref, o_ref):
    a = _silu(cond_ref[...]).astype(BF16)
    o_ref[0] = _dot(a, w_ref[0].astype(BF16)) + b_ref[0]


def _modulation(cond, ada_w, ada_b):
    r, d = cond.shape
    depth, _, n = ada_w.shape
    tn = 512
    return pl.pallas_call(
        _modulation_kernel,
        grid=(depth, n // tn),
        in_specs=[pl.BlockSpec((r, d), lambda l, j: (0, 0)),
                  pl.BlockSpec((1, d, tn), lambda l, j: (l, 0, j)),
                  pl.BlockSpec((1, 1, tn), lambda l, j: (l, 0, j))],
        out_specs=pl.BlockSpec((1, r, tn), lambda l, j: (l, 0, j)),
        out_shape=jax.ShapeDtypeStruct((depth, r, n), F32),
        compiler_params=_params(2),
        name="adaln_modulation",
    )(cond, ada_w, ada_b.reshape(depth, 1, n))


def _halo_specs(tm, l, d):
    per = tm // HALO
    last = l // HALO - 1
    return [pl.BlockSpec((1, tm, d), lambda b, i: (b, i, 0)),
            pl.BlockSpec((1, HALO, d), lambda b, i: (b, jnp.maximum(i * per - 1, 0), 0)),
            pl.BlockSpec((1, HALO, d), lambda b, i: (b, jnp.minimum((i + 1) * per, last), 0))]


def _halo_rows(x_ref, xp_ref, xn_ref, nw, sc, sh):
    i = pl.program_id(1)
    has_prev = (i > 0).astype(F32)
    has_next = (i < pl.num_programs(1) - 1).astype(F32)
    hp = _norm_mod(xp_ref[0], nw, sc, sh) * has_prev
    hm = _norm_mod(x_ref[0], nw, sc, sh)
    hn = _norm_mod(xn_ref[0], nw, sc, sh) * has_next
    return jnp.concatenate([hp, hm, hn], axis=0).astype(BF16)


def _conv3(u_ref, tm, w, b):
    return (u_ref[pl.ds(HALO - 1, tm), :] * w[0:1] + u_ref[pl.ds(HALO, tm), :] * w[1:2]
            + u_ref[pl.ds(HALO + 1, tm), :] * w[2:3] + b)


def _conv3_rows(u, tm, w, b):
    rows = u.shape[0]
    y = pltpu.roll(u, 1, 0) * w[0:1] + u * w[1:2] + pltpu.roll(u, rows - 1, 0) * w[2:3] + b
    return y[HALO:HALO + tm]


def _ffn_kernel(x_ref, xp_ref, xn_ref, sc_ref, sh_ref, g_ref, nw_ref, wu_ref, cw_ref, cb_ref, wd_ref, o_ref,
                h_buf, a_buf, *, tm):
    f, fc = FFN_HIDDEN, FFN_COL_CHUNK
    h_buf[...] = _halo_rows(x_ref, xp_ref, xn_ref, nw_ref[...], sc_ref[0], sh_ref[0])
    for c in range(f // fc):
        gs = slice(c * fc, (c + 1) * fc)
        vs = slice(f + c * fc, f + (c + 1) * fc)
        gate = _conv3_rows(_dot(h_buf[...], wu_ref[:, gs]), tm, cw_ref[:, gs], cb_ref[:, gs])
        val = _conv3_rows(_dot(h_buf[...], wu_ref[:, vs]), tm, cw_ref[:, vs], cb_ref[:, vs])
        a_buf[:, gs] = (_silu(gate) * val).astype(BF16)
    for n in range(D_MODEL // fc):
        cs = slice(n * fc, (n + 1) * fc)
        o_ref[0, :, cs] = x_ref[0, :, cs] + g_ref[0, :, cs] * _dot(a_buf[...], wd_ref[:, cs])


def _conv_ffn_residual(x, sc, sh, g, nw, wts, tm):
    b, l, d = x.shape
    mod = pl.BlockSpec((1, 1, d), _mod_index(sc.shape[0]))
    once = lambda a: pl.BlockSpec(a.shape, lambda b_, i: (0, 0), pipeline_mode=pl.Buffered(1))
    return pl.pallas_call(
        functools.partial(_ffn_kernel, tm=tm),
        grid=(b, l // tm),
        in_specs=_halo_specs(tm, l, d) + [mod, mod, mod, pl.BlockSpec((1, d), lambda b_, i: (0, 0))]
        + [once(w) for w in wts],
        out_specs=pl.BlockSpec((1, tm, d), lambda b_, i: (b_, i, 0)),
        out_shape=jax.ShapeDtypeStruct((b, l, d), F32),
        scratch_shapes=[pltpu.VMEM((tm + 2 * HALO, d), BF16), pltpu.VMEM((tm, FFN_HIDDEN), BF16)],
        compiler_params=_params(2),
        name="conv_ffn",
    )(x, x, x, sc, sh, g, nw, *wts)


def _ffn_weights(w_up, conv_w, conv_b, w_down):
    return w_up.astype(BF16), conv_w, conv_b[None], w_down.astype(BF16)


def _out_proj_kernel(a_ref, x_ref, g_ref, w_ref, o_ref):
    o_ref[0] = x_ref[0] + g_ref[0] * _dot(a_ref[0].astype(BF16), w_ref[...])


def _out_proj_residual(a, x, g, w, tm):
    b, l, d = x.shape
    k = a.shape[-1]
    return pl.pallas_call(
        _out_proj_kernel,
        grid=(b, l // tm),
        in_specs=[pl.BlockSpec((1, tm, k), lambda b_, i: (b_, i, 0)),
                  pl.BlockSpec((1, tm, d), lambda b_, i: (b_, i, 0)),
                  pl.BlockSpec((1, 1, d), _mod_index(g.shape[0])),
                  pl.BlockSpec((k, d), lambda b_, i: (0, 0))],
        out_specs=pl.BlockSpec((1, tm, d), lambda b_, i: (b_, i, 0)),
        out_shape=jax.ShapeDtypeStruct((b, l, d), F32),
        compiler_params=_params(2),
        name="out_proj_residual",
    )(a, x, g, w)


GLA_W1_COLS = GLA_IN + LANES


def _gla_proj_kernel(x_ref, sc_ref, sh_ref, nw_ref, w1_ref, w2_ref, bg_ref, qk_ref, vr_ref, lg_ref):
    h = _norm_mod(x_ref[0], nw_ref[...], sc_ref[0], sh_ref[0]).astype(BF16)
    p = _dot(h, w1_ref[...])
    qk_ref[0] = p[:, :2 * GLA_DK]
    vr_ref[0] = p[:, 2 * GLA_DK:GLA_IN].astype(BF16)
    z = _dot(p[:, GLA_IN:].astype(BF16), w2_ref[...]) + bg_ref[...]
    lg_ref[0] = jax.nn.log_sigmoid(z) * (1.0 / GLA_TAU)


def _gla_project(x, sc, sh, nw, w1, w2, bg, tm):
    b, l, d = x.shape
    mod = pl.BlockSpec((1, 1, d), _mod_index(sc.shape[0]))
    const = lambda a: pl.BlockSpec(a.shape, lambda b_, i: (0, 0))
    return pl.pallas_call(
        _gla_proj_kernel,
        grid=(b, l // tm),
        in_specs=[pl.BlockSpec((1, tm, d), lambda b_, i: (b_, i, 0)), mod, mod, const(nw), const(w1), const(w2),
                  const(bg)],
        out_specs=[pl.BlockSpec((1, tm, 2 * GLA_DK), lambda b_, i: (b_, i, 0)),
                   pl.BlockSpec((1, tm, 2 * GLA_DV), lambda b_, i: (b_, i, 0)),
                   pl.BlockSpec((1, tm, 2 * GLA_DK), lambda b_, i: (b_, i, 0))],
        out_shape=[jax.ShapeDtypeStruct((b, l, 2 * GLA_DK), F32), jax.ShapeDtypeStruct((b, l, 2 * GLA_DV), BF16),
                   jax.ShapeDtypeStruct((b, l, 2 * GLA_DK), F32)],
        compiler_params=_params(2),
        name="gla_in_proj",
    )(x, sc, sh, nw, w1, w2, bg)


def _gla_tables(c):
    n_lev = int(np.log2(c))
    idx = np.arange(c)
    i, j = idx[:, None], idx[None, :]
    mask = np.zeros((2, n_lev + 1, c, c), np.float32)
    sign = np.zeros((2, n_lev, c, LANES), np.float32)
    step = np.zeros((2, n_lev, c, LANES), np.float32)
    for d in range(2):
        p = idx if d == 0 else c - 1 - idx
        pi, pj = p[:, None], p[None, :]
        mask[d, 0] = np.eye(c)
        for lev in range(n_lev):
            m = c >> (lev + 1)
            upper = (p % (2 * m)) >= m
            sign[d, lev] = np.where(upper, 1.0, -1.0)[:, None]
            mask[d, lev + 1] = (pi // (2 * m) == pj // (2 * m)) & ((pi % (2 * m)) >= m) & ((pj % (2 * m)) < m)
            step[d, lev] = (p >= (1 << lev)).astype(np.float32)[:, None]
    return jnp.asarray(sign), jnp.asarray(step), jnp.asarray(mask)


def _gla_exponents(lg, g_buf, sign_ref, step_ref, d, c, n_lev):
    back = d == 1
    g = lg
    for s in range(n_lev):
        sh = 1 << s
        if sh < 8:
            g = g + step_ref[d, s] * pltpu.roll(g, (c - sh) if back else sh, 0)
        elif back:
            g = jnp.concatenate([g[:c - sh] + g[sh:], g[c - sh:]], axis=0)
        else:
            g = jnp.concatenate([g[:sh], g[sh:] + g[:c - sh]], axis=0)
    g_buf[...] = g
    total = g_buf[0:1, :] if back else g_buf[c - 1:c, :]
    row = lax.broadcasted_iota(jnp.int32, (c, LANES), 0)
    levels = []
    for lev in range(n_lev):
        m = c >> (lev + 1)
        if m == 1:
            levels.append(jnp.maximum(sign_ref[d, lev], 0.0) * lg)
            continue
        if 2 * m >= 8:
            ref_rows = [jnp.broadcast_to(g_buf[ob * 2 * m + (m if back else m - 1):ob * 2 * m + (m if back else m - 1) + 1, :],
                                         (2 * m, LANES)) for ob in range(c // (2 * m))]
            big = jnp.concatenate(ref_rows, axis=0) if len(ref_rows) > 1 else ref_rows[0]
        else:
            tgt = m if back else m - 1
            pos = row % (2 * m)
            big = g
            for k in range(2 * m):
                if k != tgt:
                    big = jnp.where(pos == k, pltpu.roll(g, (k - tgt) % c, 0), big)
        levels.append(sign_ref[d, lev] * (g - big))
    return g, total - g, total, levels


def _gla_scan_kernel(qkf_ref, qkb_ref, vf_ref, vb_ref, lgf_ref, lgb_ref, sign_ref, step_ref, mask_ref, s0_ref,
                     of_ref, ob_ref, sT_ref, state, g_buf, *, c, n_lev):
    @pl.when(pl.program_id(1) == 0)
    def _():
        state[...] = s0_ref[:, 0]

    dirs = ((qkf_ref, vf_ref, lgf_ref, of_ref), (qkb_ref, vb_ref, lgb_ref, ob_ref))
    for d, (qk_ref, v_ref, lg_ref, o_ref) in enumerate(dirs):
        for h in range(GLA_HEADS):
            ks = slice(h * GLA_HK, (h + 1) * GLA_HK)
            kks = slice(GLA_DK + h * GLA_HK, GLA_DK + (h + 1) * GLA_HK)
            vs = slice(h * GLA_HV, (h + 1) * GLA_HV)
            q = qk_ref[0, :, ks] * (GLA_HK ** -0.5)
            k = qk_ref[0, :, kks]
            vb = v_ref[0, :, vs]
            g_in, g_out, g_tot, g_lev = _gla_exponents(lg_ref[0, :, ks], g_buf.at[d, h], sign_ref, step_ref, d, c, n_lev)
            att = mask_ref[d, 0] * _dot_nt(q.astype(BF16), k.astype(BF16))
            for lev in range(n_lev):
                e = jnp.exp(g_lev[lev])
                att = att + mask_ref[d, lev + 1] * _dot_nt((q * e).astype(BF16), (k * e).astype(BF16))
            st = state[d, h]
            o = _dot(att.astype(BF16), vb) + _dot_nt((q * jnp.exp(g_in)).astype(BF16), st.astype(BF16))
            o_ref[0, :, vs] = o.astype(o_ref.dtype)
            st = st * jnp.exp(g_tot) + _dot_tn(vb, (k * jnp.exp(g_out)).astype(BF16))
            state[d, h] = st
            sT_ref[d, 0, h] = st


def _gla_scan(qk, vr, lg, tables, s0):
    b, l, _ = qk.shape
    c = SCAN_CHUNK
    n = l // c
    sign, step, mask = tables
    n_lev = mask.shape[1] - 1
    kern = functools.partial(_gla_scan_kernel, c=c, n_lev=n_lev)
    whole = lambda a: pl.BlockSpec(a.shape, lambda b_, i: (0,) * a.ndim)
    fwd = lambda blk: (lambda b_, i: (b_, i, blk))
    bwd = lambda blk: (lambda b_, i: (b_, n - 1 - i, blk))
    st_spec = pl.BlockSpec((2, 1, GLA_HEADS, GLA_HV, GLA_HK), lambda b_, i: (0, b_, 0, 0, 0))
    return pl.pallas_call(
        kern,
        grid=(b, n),
        in_specs=[pl.BlockSpec((1, c, 2 * GLA_DK), fwd(0)), pl.BlockSpec((1, c, 2 * GLA_DK), bwd(0)),
                  pl.BlockSpec((1, c, GLA_DV), fwd(0)), pl.BlockSpec((1, c, GLA_DV), bwd(0)),
                  pl.BlockSpec((1, c, GLA_DK), fwd(0)), pl.BlockSpec((1, c, GLA_DK), bwd(1)),
                  whole(sign), whole(step), whole(mask), st_spec],
        out_specs=[pl.BlockSpec((1, c, GLA_DV), fwd(0)), pl.BlockSpec((1, c, GLA_DV), bwd(0)), st_spec],
        out_shape=[jax.ShapeDtypeStruct((b, l, GLA_DV), BF16), jax.ShapeDtypeStruct((b, l, GLA_DV), BF16),
                   jax.ShapeDtypeStruct((2, b, GLA_HEADS, GLA_HV, GLA_HK), F32)],
        scratch_shapes=[pltpu.VMEM((2, GLA_HEADS, GLA_HV, GLA_HK), F32),
                        pltpu.VMEM((2, GLA_HEADS, c, GLA_HK), F32)],
        compiler_params=_params(2),
        name="gla_scan",
    )(qk, qk, vr, vr, lg, lg, sign, step, mask, s0)


def _gla_finish_kernel(of_ref, ob_ref, r_ref, x_ref, g_ref, nw_ref, w_ref, o_ref):
    o = of_ref[0].astype(F32) + ob_ref[0].astype(F32)
    heads = []
    for h in range(GLA_HEADS):
        oh = o[:, h * GLA_HV:(h + 1) * GLA_HV]
        ms = jnp.mean(oh * oh, axis=-1, keepdims=True)
        heads.append(oh * lax.rsqrt(ms + EPS) * nw_ref[...])
    y = jnp.concatenate(heads, axis=-1) * _silu(r_ref[0].astype(F32))
    o_ref[0] = x_ref[0] + g_ref[0] * _dot(y.astype(BF16), w_ref[...])


def _gla_finish(o_f, o_b, vr, x, g, nw, w, tm):
    b, l, d = x.shape
    return pl.pallas_call(
        _gla_finish_kernel,
        grid=(b, l // tm),
        in_specs=[pl.BlockSpec((1, tm, GLA_DV), lambda b_, i: (b_, i, 0)),
                  pl.BlockSpec((1, tm, GLA_DV), lambda b_, i: (b_, i, 0)),
                  pl.BlockSpec((1, tm, GLA_DV), lambda b_, i: (b_, i, 1)),
                  pl.BlockSpec((1, tm, d), lambda b_, i: (b_, i, 0)),
                  pl.BlockSpec((1, 1, d), _mod_index(g.shape[0])),
                  pl.BlockSpec((1, GLA_HV), lambda b_, i: (0, 0)),
                  pl.BlockSpec((GLA_DV, d), lambda b_, i: (0, 0))],
        out_specs=pl.BlockSpec((1, tm, d), lambda b_, i: (b_, i, 0)),
        out_shape=jax.ShapeDtypeStruct((b, l, d), F32),
        compiler_params=_params(2),
        name="gla_finish",
    )(o_f, o_b, vr, x, g, nw, w)


def _gla_weights(w_in, w_g1, w_g2, b_g, norm_w, w_out):
    d = w_in.shape[0]
    r = GLA_GATE_RANK
    w1 = jnp.concatenate([w_in, w_g1[0], w_g1[1], jnp.zeros((d, LANES - 2 * r), F32)], axis=1).astype(BF16)
    w2 = jnp.zeros((LANES, 2 * GLA_DK), F32)
    w2 = w2.at[0:r, :GLA_DK].set(w_g2[0]).at[r:2 * r, GLA_DK:].set(w_g2[1]).astype(BF16)
    return w1, w2, b_g.reshape(1, 2 * GLA_DK), norm_w.reshape(1, GLA_HV), w_out.astype(BF16)


def _gla_mixer(xc, xl, mod_c, mod_l, nw, wts, tables, ctx_out, tm_l, tm_c):
    w1, w2, bg, norm_w, w_out = wts
    (csc, csh, cg), (sc, sh, g) = mod_c, mod_l
    qkc, vrc, lgc = _gla_project(xc, csc, csh, nw, w1, w2, bg, tm_c)
    qkl, vrl, lgl = _gla_project(xl, sc, sh, nw, w1, w2, bg, tm_l)
    zero = jnp.zeros((2, xc.shape[0], GLA_HEADS, GLA_HV, GLA_HK), F32)
    ocf, ocb, s_ctx = _gla_scan(qkc, vrc, lgc, tables, zero)
    olf, olb, _ = _gla_scan(qkl, vrl, lgl, tables, s_ctx)
    xl = _gla_finish(olf, olb, vrl, xl, g, norm_w, w_out, tm_l)
    if ctx_out:
        xc = _gla_finish(ocf, ocb, vrc, xc, cg, norm_w, w_out, tm_c)
    return xc, xl


MLA_HEAD_PAD = 2 * LANES
MLA_V_PAD = 2 * LANES


def _rot_cols(w):
    half = MLA_ROPE // 2
    return jnp.concatenate([-w[..., half:], w[..., :half]], axis=-1)


def _swap_halves(w):
    half = MLA_ROPE // 2
    return jnp.concatenate([w[..., half:], w[..., :half]], axis=-1)


def _mla_proj_kernel(x_ref, sc_ref, sh_ref, nw_ref, win_ref, qan_ref, kvan_ref, wuq_ref, wukv_ref,
                     qna_ref, qnb_ref, kna_ref, knb_ref, cs_ref, q_ref, k_ref, v_ref):
    h = _norm_mod(x_ref[0], nw_ref[...], sc_ref[0], sh_ref[0]).astype(BF16)
    p = _dot(h, win_ref[...])
    cq = p[:, :MLA_Q_RANK]
    ckv = p[:, MLA_Q_RANK:MLA_Q_RANK + MLA_KV_RANK]
    kpe = p[:, MLA_Q_RANK + MLA_KV_RANK:]
    cq = cq * lax.rsqrt(jnp.mean(cq * cq, axis=-1, keepdims=True) + EPS) * qan_ref[...]
    ckv = ckv * lax.rsqrt(jnp.mean(ckv * ckv, axis=-1, keepdims=True) + EPS) * kvan_ref[...]
    qall = _dot(cq.astype(BF16), wuq_ref[...])
    kvall = _dot(ckv.astype(BF16), wukv_ref[...])
    cs = cs_ref[...]
    lane = lax.broadcasted_iota(jnp.int32, (1, LANES), 1)
    keep = (lane < MLA_ROPE).astype(F32)
    kpe_ss = 0.5 * jnp.sum(kpe * kpe, axis=-1, keepdims=True)
    scale = MLA_QK ** -0.5 * np.log2(np.e)

    def head(nope, pe, na, nb, mult):
        ss = jnp.sum(nope * nope, axis=-1, keepdims=True) + 0.5 * jnp.sum(pe * pe, axis=-1, keepdims=True)
        s = lax.rsqrt(ss * (1.0 / MLA_QK) + EPS)
        t = pe * s * nb * cs
        rope = (t + pltpu.roll(t, MLA_ROPE, 1)) * keep
        return jnp.concatenate([nope * s * na * mult, rope * mult], axis=-1).astype(BF16)

    for hd in range(MLA_HEADS):
        qh = qall[:, hd * MLA_HEAD_PAD:(hd + 1) * MLA_HEAD_PAD]
        q_ref[0, hd] = head(qh[:, :LANES], qh[:, LANES:], qna_ref[...], qnb_ref[...], scale)
        kh = kvall[:, hd * 2 * LANES:hd * 2 * LANES + LANES]
        ss = jnp.sum(kh * kh, axis=-1, keepdims=True) + kpe_ss
        s = lax.rsqrt(ss * (1.0 / MLA_QK) + EPS)
        t = kpe * s * knb_ref[...] * cs
        rope = (t + pltpu.roll(t, MLA_ROPE, 1)) * keep
        k_ref[0, hd] = jnp.concatenate([kh * s * kna_ref[...], rope], axis=-1).astype(BF16)
        vh = kvall[:, hd * 2 * LANES + LANES:(hd + 1) * 2 * LANES]
        v_ref[0, hd] = jnp.concatenate([vh, jnp.ones_like(vh)], axis=-1).astype(BF16)


def _mla_project(x, sc, sh, nw, wts, cs, tm):
    b, l, d = x.shape
    win, qan, kvan, wuq, wukv, qna, qnb, kna, knb = wts[:9]
    mod = pl.BlockSpec((1, 1, d), _mod_index(sc.shape[0]))
    const = lambda a: pl.BlockSpec(a.shape, lambda b_, i: (0, 0))
    cs_spec = (pl.BlockSpec((tm, LANES), lambda b_, i: (i, 0)) if cs.shape[0] == l
               else pl.BlockSpec((tm, LANES), lambda b_, i: (0, 0)))
    hspec = lambda w: pl.BlockSpec((1, MLA_HEADS, tm, w), lambda b_, i: (b_, 0, i, 0))
    return pl.pallas_call(
        _mla_proj_kernel,
        grid=(b, l // tm),
        in_specs=[pl.BlockSpec((1, tm, d), lambda b_, i: (b_, i, 0)), mod, mod, const(nw), const(win), const(qan),
                  const(kvan), const(wuq), const(wukv), const(qna), const(qnb), const(kna), const(knb), cs_spec],
        out_specs=[hspec(MLA_HEAD_PAD), hspec(MLA_HEAD_PAD), hspec(MLA_V_PAD)],
        out_shape=[jax.ShapeDtypeStruct((b, MLA_HEADS, l, MLA_HEAD_PAD), BF16),
                   jax.ShapeDtypeStruct((b, MLA_HEADS, l, MLA_HEAD_PAD), BF16),
                   jax.ShapeDtypeStruct((b, MLA_HEADS, l, MLA_V_PAD), BF16)],
        compiler_params=_params(2),
        name="mla_project",
    )(x, sc, sh, nw, win, qan, kvan, wuq, wukv, qna, qnb, kna, knb, cs)


def _flash_kernel(*refs, tk, n_kv, has_ctx):
    if has_ctx:
        q_ref, k_ref, v_ref, kc_ref, vc_ref, o_ref = refs
    else:
        q_ref, k_ref, v_ref, o_ref = refs
    q = q_ref[0, 0]

    def update(carry, kj, vj):
        m, acc = carry
        s = _dot_nt(q, kj)
        m_new = jnp.maximum(m, jnp.max(s, axis=-1, keepdims=True))
        p = jnp.exp2(s - m_new)
        return m_new, jnp.exp2(m - m_new) * acc + _dot(p.astype(BF16), vj)

    tq = q.shape[0]
    carry = (jnp.full((tq, 1), -jnp.inf, F32), jnp.zeros((tq, MLA_V_PAD), F32))
    if has_ctx:
        carry = update(carry, kc_ref[0, 0], vc_ref[0, 0])
    for j in range(n_kv):
        carry = update(carry, k_ref[0, 0, j * tk:(j + 1) * tk, :], v_ref[0, 0, j * tk:(j + 1) * tk, :])
    acc = carry[1]
    o_ref[0] = (acc[:, :MLA_V] / acc[:, MLA_V:]).astype(o_ref.dtype)


def _flash_attention(q, k, v, kc, vc, tq, tk):
    b, hds, l, dq = q.shape
    lk = k.shape[2]
    has_ctx = kc is not None
    kern = functools.partial(_flash_kernel, tk=tk, n_kv=lk // tk, has_ctx=has_ctx)
    whole = lambda a: pl.BlockSpec((1, 1) + a.shape[2:], lambda b_, h, i: (b_, h, 0, 0))
    in_specs = [pl.BlockSpec((1, 1, tq, dq), lambda b_, h, i: (b_, h, i, 0)), whole(k), whole(v)]
    args = [q, k, v]
    if has_ctx:
        in_specs += [whole(kc), whole(vc)]
        args += [kc, vc]
    return pl.pallas_call(
        kern,
        grid=(b, hds, l // tq),
        in_specs=in_specs,
        out_specs=pl.BlockSpec((1, tq, MLA_V), lambda b_, h, i: (b_, i, h)),
        out_shape=jax.ShapeDtypeStruct((b, l, hds * MLA_V), BF16),
        compiler_params=_params(3),
        name="mla_flash_attention",
    )(*args)


def _mla_weights(w_in, q_a_norm, w_uq, kv_a_norm, w_ukv, q_norm, k_norm, w_out):
    kpe = w_in[:, MLA_Q_RANK + MLA_KV_RANK:]
    win = jnp.concatenate([w_in, _rot_cols(kpe)], axis=1).astype(BF16)
    wq = w_uq.reshape(MLA_Q_RANK, MLA_HEADS, MLA_QK)
    wq = jnp.concatenate([wq, _rot_cols(wq[..., MLA_NOPE:])], axis=-1)
    wuq = wq.reshape(MLA_Q_RANK, MLA_HEADS * MLA_HEAD_PAD).astype(BF16)
    gains = lambda nrm: (nrm[None, :MLA_NOPE],
                         jnp.concatenate([nrm[MLA_NOPE:], _swap_halves(nrm[MLA_NOPE:])])[None])
    qna, qnb = gains(q_norm)
    kna, knb = gains(k_norm)
    return (win, q_a_norm[None], kv_a_norm[None], wuq, w_ukv.astype(BF16), qna, qnb, kna, knb, w_out.astype(BF16))


def _rope_table(l):
    rows = l // GRID_W
    pos_r = jnp.repeat(jnp.arange(rows, dtype=F32), GRID_W)
    pos_c = jnp.tile(jnp.arange(GRID_W, dtype=F32), rows)
    n_freq = MLA_ROPE // 4
    inv = ROPE_THETA ** (-jnp.arange(n_freq, dtype=F32) / n_freq)
    ang = jnp.concatenate([pos_r[:, None] * inv, pos_c[:, None] * inv], axis=-1)
    return jnp.concatenate([jnp.cos(ang), jnp.cos(ang), jnp.sin(ang), jnp.sin(ang)], axis=-1)


def _mla_mixer(xc, xl, mod_c, mod_l, nw, wts, ctx_out, tm_l, tm_c):
    (csc, csh, cg), (sc, sh, g) = mod_c, mod_l
    w_out = wts[9]
    lc, l = xc.shape[1], xl.shape[1]
    no_rope = jnp.concatenate([jnp.ones((tm_c, LANES // 2), F32), jnp.zeros((tm_c, LANES // 2), F32)], axis=-1)
    qc, kc, vc = _mla_project(xc, csc, csh, nw, wts, no_rope, tm_c)
    ql, kl, vl = _mla_project(xl, sc, sh, nw, wts, _rope_table(l), tm_l)
    tq = min(1024, l)
    al = _flash_attention(ql, kl, vl, kc, vc, tq, min(512, l))
    xl = _out_proj_residual(al, xl, g, w_out, tm_l)
    if ctx_out:
        ac = _flash_attention(qc, kc, vc, None, None, lc, lc)
        xc = _out_proj_residual(ac, xc, cg, w_out, tm_c)
    return xc, xl


def _ssd_proj_kernel(x_ref, xp_ref, xn_ref, sc_ref, sh_ref, nw_ref, wz_ref, wx_ref, wdt_ref, cw_ref, cb_ref,
                     dtb_ref, z_ref, xbc_ref, dt_ref, h_buf, *, tm):
    cc = SSD_COL_CHUNK
    h_buf[...] = _halo_rows(x_ref, xp_ref, xn_ref, nw_ref[...], sc_ref[0], sh_ref[0])
    hm = h_buf[pl.ds(HALO, tm), :]
    z_ref[0] = _dot(hm, wz_ref[...]).astype(z_ref.dtype)
    dt = jax.nn.softplus(_dot(hm, wdt_ref[...]) + dtb_ref[...])
    dt_ref[0, 0] = dt[:, :LANES]
    dt_ref[1, 0] = dt[:, LANES:]
    for c in range(SSM_XBC // cc):
        cs = slice(c * cc, (c + 1) * cc)
        u = _conv3_rows(_dot(h_buf[...], wx_ref[c]), tm, cw_ref[:, cs], cb_ref[:, cs])
        xbc_ref[0, :, cs] = _silu(u).astype(xbc_ref.dtype)


def _ssd_project(x, sc, sh, nw, wts, tm):
    b, l, d = x.shape
    wz, wx, wdt, cw, cb, dtb = wts[:6]
    mod = pl.BlockSpec((1, 1, d), _mod_index(sc.shape[0]))
    const2 = lambda a: pl.BlockSpec(a.shape, lambda b_, i: (0,) * a.ndim)
    kern = functools.partial(_ssd_proj_kernel, tm=tm)
    return pl.pallas_call(
        kern,
        grid=(b, l // tm),
        in_specs=_halo_specs(tm, l, d) + [mod, mod, const2(nw), const2(wz), const2(wx), const2(wdt), const2(cw),
                                           const2(cb), const2(dtb)],
        out_specs=[pl.BlockSpec((1, tm, SSM_INNER), lambda b_, i: (b_, i, 0)),
                   pl.BlockSpec((1, tm, SSM_XBC), lambda b_, i: (b_, i, 0)),
                   pl.BlockSpec((2, 1, tm, LANES), lambda b_, i: (0, b_, i, 0))],
        out_shape=[jax.ShapeDtypeStruct((b, l, SSM_INNER), BF16),
                   jax.ShapeDtypeStruct((b, l, SSM_XBC), BF16),
                   jax.ShapeDtypeStruct((2, b, l, LANES), F32)],
        scratch_shapes=[pltpu.VMEM((tm + 2 * HALO, d), BF16)],
        compiler_params=_params(2),
        name="ssd_in_proj_conv",
    )(x, x, x, sc, sh, nw, wz, wx, wdt, cw, cb, dtb)


def _ssd_tables(c):
    idx = np.arange(c)
    i, t = idx[:, None], idx[None, :]
    ones = np.ones((8, c), np.float32)
    fwd = np.concatenate([(t <= i).astype(np.float32), ones], 0)
    bwd = np.concatenate([(t >= i).astype(np.float32), ones], 0)
    mask = np.stack([(t <= i), (t >= i)]).astype(np.float32)
    h = np.arange(SSM_HEADS)
    e_seg = np.zeros((LANES, SSM_HEADS, c), np.float32)
    e_seg[h, h, :] = 1.0
    e_seg[SSM_HEADS + h, h, :] = 1.0
    e_x = np.zeros((LANES, 2, SSM_HEADS, SSM_HEADDIM), np.float32)
    for q in range(2):
        e_x[2 * q * SSM_HEADS + h, q, h, :] = 1.0
        e_x[(2 * q + 1) * SSM_HEADS + h, q, h, :] = 1.0
    return (jnp.asarray(np.stack([fwd, bwd]), BF16), jnp.asarray(mask, F32),
            jnp.asarray(e_seg.reshape(LANES, SSM_HEADS * c), BF16),
            jnp.asarray(e_x.reshape(LANES, 2 * SSM_INNER), BF16))


def _pack_hi_lo(parts):
    out = None
    for n, v in enumerate(parts):
        hi = v.astype(BF16).astype(F32)
        for j, piece in enumerate((hi, v - hi)):
            shift = (2 * n + j) * SSM_HEADS
            piece = piece if shift == 0 else pltpu.roll(piece, shift, 1)
            out = piece if out is None else out + piece
    return out.astype(BF16)


def _ssd_scan_kernel(xf_ref, bf_ref, cf_ref, dtf_ref, xb_ref, bb_ref, cb_ref, dtb_ref, a_ref, tri_ref, mask_ref,
                     eseg_ref, ex_ref, s0_ref, yf_ref, yb_ref, sf_ref, state, *, c):
    @pl.when(pl.program_id(1) == 0)
    def _():
        state[...] = s0_ref[:, 0]

    lane = lax.broadcasted_iota(jnp.int32, (1, LANES), 1)
    valid = (lane < SSM_HEADS).astype(F32)
    left = lane < SSM_HEADDIM
    per_group = SSM_HEADS // SSM_GROUPS
    gw = per_group * SSM_HEADDIM
    dirs = ((xf_ref, bf_ref, cf_ref, dtf_ref, yf_ref), (xb_ref, bb_ref, cb_ref, dtb_ref, yb_ref))
    for d, (x_ref, b_ref, c_ref, dt_ref, y_ref) in enumerate(dirs):
        dt = dt_ref[0, 0] * valid
        cums = _split_dot(tri_ref[d], dt * a_ref[d])
        cum = cums[:c]
        total = cums[c:c + 1]
        e_in = jnp.exp(cum) * valid
        w_out = jnp.exp(total - cum) * dt
        e_tot = jnp.exp(total)
        cum_pk = _pack_hi_lo([cum])
        ew_pk = _pack_hi_lo([e_in, w_out])
        cum_t = jnp.transpose(cum)
        dt_t = jnp.transpose(dt)
        causal = mask_ref[d] > 0.5
        for g in range(SSM_GROUPS):
            bm = b_ref[0, :, g * SSM_STATE:(g + 1) * SSM_STATE]
            cm = c_ref[0, :, g * SSM_STATE:(g + 1) * SSM_STATE]
            scores = _dot_nt(cm, bm)
            sg = state[d, g * gw:(g + 1) * gw, :]
            y_in = _dot_nt(cm, sg.astype(BF16))
            x_g = x_ref[0, :, g * gw:(g + 1) * gw]
            cum_col = _dot(cum_pk, eseg_ref[:, g * per_group * c:(g + 1) * per_group * c])
            e_in_x = _dot(ew_pk, ex_ref[:, g * gw:(g + 1) * gw])
            w_out_x = _dot(ew_pk, ex_ref[:, SSM_INNER + g * gw:SSM_INNER + (g + 1) * gw])
            for pr in range(per_group // 2):
                h0 = g * per_group + 2 * pr
                col = g * gw + pr * LANES
                ms = []
                for j in range(2):
                    h = h0 + j
                    seg = jnp.exp(cum_col[:, (2 * pr + j) * c:(2 * pr + j + 1) * c] - cum_t[h:h + 1, :])
                    ms.append((scores * jnp.where(causal, seg, 0.0) * dt_t[h:h + 1, :]).astype(BF16))
                xp = x_g[:, pr * LANES:(pr + 1) * LANES]
                zero = jnp.zeros_like(xp)
                xbd = jnp.concatenate([jnp.where(left, xp, zero), jnp.where(left, zero, xp)], axis=0)
                y = _dot(jnp.concatenate(ms, axis=1), xbd)
                y = y + e_in_x[:, pr * LANES:(pr + 1) * LANES] * y_in[:, pr * LANES:(pr + 1) * LANES]
                y_ref[0, :, col:col + LANES] = y.astype(y_ref.dtype)
            wx = (x_g.astype(F32) * w_out_x).astype(BF16)
            upd = _dot_tn(wx, bm)
            dec = jnp.concatenate(
                [jnp.broadcast_to(e_tot[:, g * per_group + j:g * per_group + j + 1], (SSM_HEADDIM, SSM_STATE))
                 for j in range(per_group)], axis=0)
            state[d, g * gw:(g + 1) * gw, :] = sg * dec + upd
    sf_ref[:, 0] = state[...]


def _ssd_scan(xbc, dt2, a_rows, tables, s0):
    b, l, _ = xbc.shape
    c = SCAN_CHUNK
    n = l // c
    tri, mask, e_seg, e_x = tables
    xblk = SSM_INNER // SSM_BC
    kern = functools.partial(_ssd_scan_kernel, c=c)
    fwd = lambda blk: (lambda b_, i: (b_, i, blk))
    bwd = lambda blk: (lambda b_, i: (b_, n - 1 - i, blk))
    stream = lambda at: [pl.BlockSpec((1, c, SSM_INNER), at(0)), pl.BlockSpec((1, c, SSM_BC), at(xblk)),
                         pl.BlockSpec((1, c, SSM_BC), at(xblk + 1))]
    whole = lambda a: pl.BlockSpec(a.shape, lambda b_, i: (0,) * a.ndim)
    st_spec = pl.BlockSpec((2, 1, SSM_INNER, SSM_STATE), lambda b_, i: (0, b_, 0, 0))
    return pl.pallas_call(
        kern,
        grid=(b, n),
        in_specs=(stream(fwd) + [pl.BlockSpec((1, 1, c, LANES), lambda b_, i: (0, b_, i, 0))]
                  + stream(bwd) + [pl.BlockSpec((1, 1, c, LANES), lambda b_, i: (1, b_, n - 1 - i, 0))]
                  + [whole(a_rows), whole(tri), whole(mask), whole(e_seg), whole(e_x), st_spec]),
        out_specs=[pl.BlockSpec((1, c, SSM_INNER), fwd(0)), pl.BlockSpec((1, c, SSM_INNER), bwd(0)), st_spec],
        out_shape=[jax.ShapeDtypeStruct((b, l, SSM_INNER), BF16), jax.ShapeDtypeStruct((b, l, SSM_INNER), BF16),
                   jax.ShapeDtypeStruct((2, b, SSM_INNER, SSM_STATE), F32)],
        scratch_shapes=[pltpu.VMEM((2, SSM_INNER, SSM_STATE), F32)],
        compiler_params=_params(2),
        name="ssd_scan",
    )(xbc, xbc, xbc, dt2, xbc, xbc, xbc, dt2, a_rows, tri, mask, e_seg, e_x, s0)


def _ssd_finish_kernel(yf_ref, yb_ref, xs_ref, z_ref, x_ref, g_ref, dsk_ref, nw_ref, w_ref, o_ref):
    y = yf_ref[0].astype(F32) + yb_ref[0].astype(F32) + xs_ref[0].astype(F32) * dsk_ref[...]
    y = y * _silu(z_ref[0].astype(F32))
    gw = SSM_INNER // SSM_GROUPS
    parts = []
    for gi in range(SSM_GROUPS):
        yg = y[:, gi * gw:(gi + 1) * gw]
        ms = jnp.mean(yg * yg, axis=-1, keepdims=True)
        parts.append(yg * lax.rsqrt(ms + EPS) * nw_ref[:, gi * gw:(gi + 1) * gw])
    yn = jnp.concatenate(parts, axis=-1)
    o_ref[0] = x_ref[0] + g_ref[0] * _dot(yn.astype(BF16), w_ref[...])


def _ssd_finish(y_f, y_b, xbc, z, x, g, dsk, nw, w, tm):
    b, l, d = x.shape
    return pl.pallas_call(
        _ssd_finish_kernel,
        grid=(b, l // tm),
        in_specs=[pl.BlockSpec((1, tm, SSM_INNER), lambda b_, i: (b_, i, 0)),
                  pl.BlockSpec((1, tm, SSM_INNER), lambda b_, i: (b_, i, 0)),
                  pl.BlockSpec((1, tm, SSM_INNER), lambda b_, i: (b_, i, 0)),
                  pl.BlockSpec((1, tm, SSM_INNER), lambda b_, i: (b_, i, 0)),
                  pl.BlockSpec((1, tm, d), lambda b_, i: (b_, i, 0)),
                  pl.BlockSpec((1, 1, d), _mod_index(g.shape[0])),
                  pl.BlockSpec((1, SSM_INNER), lambda b_, i: (0, 0)),
                  pl.BlockSpec((1, SSM_INNER), lambda b_, i: (0, 0)),
                  pl.BlockSpec((SSM_INNER, d), lambda b_, i: (0, 0))],
        out_specs=pl.BlockSpec((1, tm, d), lambda b_, i: (b_, i, 0)),
        out_shape=jax.ShapeDtypeStruct((b, l, d), F32),
        compiler_params=_params(2),
        name="ssd_finish",
    )(y_f, y_b, xbc, z, x, g, dsk, nw, w)


def _ssd_weights(w_in, conv_w, conv_b, dt_bias, a_log, d_skip, norm_w, w_out):
    d = w_in.shape[0]
    wz = w_in[:, :SSM_INNER].astype(BF16)
    wx = w_in[:, SSM_INNER:SSM_INNER + SSM_XBC].reshape(d, SSM_XBC // SSD_COL_CHUNK, SSD_COL_CHUNK)
    wx = jnp.transpose(wx, (1, 0, 2)).astype(BF16)
    wdt_raw = w_in[:, SSM_INNER + SSM_XBC:]
    pad = jnp.zeros((d, LANES - SSM_HEADS), F32)
    wdt = jnp.concatenate([wdt_raw[:, :SSM_HEADS], pad, wdt_raw[:, SSM_HEADS:], pad], axis=1).astype(BF16)
    bpad = jnp.zeros((LANES - SSM_HEADS,), F32)
    dtb = jnp.concatenate([dt_bias[0], bpad, dt_bias[1], bpad])[None]
    cw, cb = conv_w, conv_b[None]
    a = -jnp.exp(a_log.astype(F32))
    a_rows = jnp.concatenate([a, jnp.zeros((2, LANES - SSM_HEADS), F32)], axis=1).reshape(2, 1, LANES)
    dsk = jnp.repeat(d_skip, SSM_HEADDIM)[None]
    return wz, wx, wdt, cw, cb, dtb, a_rows, dsk, norm_w[None], w_out.astype(BF16)


def _ssd_mixer(xc, xl, mod_c, mod_l, nw, wts, tables, ctx_out, tm_l, tm_c):
    a_rows, dsk, norm_w, w_out = wts[6:]
    (csc, csh, cg), (sc, sh, g) = mod_c, mod_l
    zc, xbc_c, dtc = _ssd_project(xc, csc, csh, nw, wts, tm_c)
    zl, xbc_l, dtl = _ssd_project(xl, sc, sh, nw, wts, tm_l)
    zero = jnp.zeros((2, xc.shape[0], SSM_INNER, SSM_STATE), F32)
    ycf, ycb, s_ctx = _ssd_scan(xbc_c, dtc, a_rows, tables, zero)
    ylf, ylb, _ = _ssd_scan(xbc_l, dtl, a_rows, tables, s_ctx)
    xl = _ssd_finish(ylf, ylb, xbc_l, zl, xl, g, dsk, norm_w, w_out, tm_l)
    if ctx_out:
        xc = _ssd_finish(ycf, ycb, xbc_c, zc, xc, cg, dsk, norm_w, w_out, tm_c)
    return xc, xl


def kernel(x, c, ctx, c_ctx, ada_w, ada_b, norm_mix_w, norm_ffn_w, ffn_w_up, ffn_conv_w, ffn_conv_b, ffn_w_down, gla_w_in, gla_w_g1, gla_w_g2, gla_b_g, gla_norm_w, gla_w_out, mla_w_in, mla_q_a_norm, mla_w_uq, mla_kv_a_norm, mla_w_ukv, mla_q_norm, mla_k_norm, mla_w_out, ssm_w_in, ssm_conv_w, ssm_conv_b, ssm_dt_bias, ssm_a_log, ssm_d, ssm_norm_w, ssm_w_out):
    bsz, l, d = x.shape
    lc = ctx.shape[1]
    tm_l = min(512, l)
    tm_c = lc
    cond = jnp.concatenate([c, c_ctx[None], jnp.zeros((8 - bsz - 1, d), F32)], axis=0)
    mods = _modulation(cond, ada_w, ada_b)
    gla_tables = _gla_tables(SCAN_CHUNK)
    ssd_tables = _ssd_tables(SCAN_CHUNK)
    xl, xc = x, ctx
    for i in range(DEPTH):
        kind, j = i % N_MIXERS, i // N_MIXERS
        ctx_out = i < DEPTH - 1
        m6 = [mods[i, :, k * d:(k + 1) * d] for k in range(N_MOD)]
        lat = [m[:bsz, None, :] for m in m6]
        cx = [m[bsz:bsz + 1, None, :] for m in m6]
        nw = norm_mix_w[i][None]
        mod_l, mod_c = (lat[1], lat[0], lat[2]), (cx[1], cx[0], cx[2])
        if kind == 0:
            wts = _gla_weights(gla_w_in[j], gla_w_g1[j], gla_w_g2[j], gla_b_g[j], gla_norm_w[j], gla_w_out[j])
            xc, xl = _gla_mixer(xc, xl, mod_c, mod_l, nw, wts, gla_tables, ctx_out, tm_l, tm_c)
        elif kind == 1:
            wts = _mla_weights(mla_w_in[j], mla_q_a_norm[j], mla_w_uq[j], mla_kv_a_norm[j], mla_w_ukv[j],
                               mla_q_norm[j], mla_k_norm[j], mla_w_out[j])
            xc, xl = _mla_mixer(xc, xl, mod_c, mod_l, nw, wts, ctx_out, tm_l, tm_c)
        else:
            wts = _ssd_weights(ssm_w_in[j], ssm_conv_w[j], ssm_conv_b[j], ssm_dt_bias[j], ssm_a_log[j], ssm_d[j],
                               ssm_norm_w[j], ssm_w_out[j])
            xc, xl = _ssd_mixer(xc, xl, mod_c, mod_l, nw, wts, ssd_tables, ctx_out, tm_l, tm_c)
        fw = _ffn_weights(ffn_w_up[i], ffn_conv_w[i], ffn_conv_b[i], ffn_w_down[i])
        nfw = norm_ffn_w[i][None]
        xl = _conv_ffn_residual(xl, lat[4], lat[3], lat[5], nfw, fw, min(FFN_ROWS, l))
        if ctx_out:
            xc = _conv_ffn_residual(xc, cx[4], cx[3], cx[5], nfw, fw, tm_c)
    return xl
```

```python
import functools

import numpy as np
import jax
import jax.numpy as jnp
from jax import lax
from jax.experimental import pallas as pl
from jax.experimental.pallas import tpu as pltpu

F32 = jnp.float32
BF16 = jnp.bfloat16

D_MODEL = 1024
DEPTH = 4
GRID_W = 64
N_MIXERS = 3
N_MOD = 6
EPS = 1e-6

GLA_HEADS = 4
GLA_DK = D_MODEL // 2
GLA_DV = D_MODEL
GLA_HK = GLA_DK // GLA_HEADS
GLA_HV = GLA_DV // GLA_HEADS
GLA_GATE_RANK = 16
GLA_TAU = 16.0
GLA_IN = 2 * GLA_DK + 2 * GLA_DV

MLA_HEADS = 8
MLA_NOPE = 128
MLA_ROPE = 64
MLA_QK = MLA_NOPE + MLA_ROPE
MLA_V = 128
MLA_Q_RANK = 256
MLA_KV_RANK = 128
ROPE_THETA = 10000.0

SSM_INNER = 2 * D_MODEL
SSM_HEADDIM = 64
SSM_HEADS = SSM_INNER // SSM_HEADDIM
SSM_STATE = 128
SSM_GROUPS = 4
SSM_BC = SSM_GROUPS * SSM_STATE
SSM_XBC = SSM_INNER + 2 * SSM_BC

FFN_HIDDEN = 2816

LANES = 128
HALO = 8
VMEM_LIMIT_BYTES = 56 * 1024 * 1024
SCAN_CHUNK = 128
FFN_COL_CHUNK = 256
FFN_ROWS = 1024
SSD_COL_CHUNK = 512


def _params(n_axes):
    return pltpu.CompilerParams(dimension_semantics=("arbitrary",) * n_axes,
                                vmem_limit_bytes=VMEM_LIMIT_BYTES)


def _dot(a, b):
    return jnp.dot(a, b, preferred_element_type=F32)


def _dot_nt(a, b):
    return lax.dot_general(a, b, (((1,), (1,)), ((), ())), preferred_element_type=F32)


def _dot_tn(a, b):
    return lax.dot_general(a, b, (((0,), (0,)), ((), ())), preferred_element_type=F32)


def _split_dot(mat_bf16, v_f32):
    hi = v_f32.astype(BF16)
    lo = (v_f32 - hi.astype(F32)).astype(BF16)
    return _dot(mat_bf16, hi) + _dot(mat_bf16, lo)


def _norm_mod(x, nw, sc, sh):
    ms = jnp.mean(x * x, axis=-1, keepdims=True)
    return (x * lax.rsqrt(ms + EPS) * nw) * (1.0 + sc) + sh


def _silu(x):
    return x * jax.nn.sigmoid(x)


def _mod_index(n_mod):
    if n_mod == 1:
        return lambda b, i: (0, 0, 0)
    return lambda b, i: (b, 0, 0)


def _modulation_kernel(cond_ref, w_ref, b_ref, o_ref):
    a = _silu(cond_ref[...]).astype(BF16)
    o_ref[0] = _dot(a, w_ref[0].astype(BF16)) + b_ref[0]


def _modulation(cond, ada_w, ada_b):
    r, d = cond.shape
    depth, _, n = ada_w.shape
    tn = 512
    return pl.pallas_call(
        _modulation_kernel,
        grid=(depth, n // tn),
        in_specs=[pl.BlockSpec((r, d), lambda l, j: (0, 0)),
                  pl.BlockSpec((1, d, tn), lambda l, j: (l, 0, j)),
                  pl.BlockSpec((1, 1, tn), lambda l, j: (l, 0, j))],
        out_specs=pl.BlockSpec((1, r, tn), lambda l, j: (l, 0, j)),
        out_shape=jax.ShapeDtypeStruct((depth, r, n), F32),
        compiler_params=_params(2),
        name="adaln_modulation",
    )(cond, ada_w, ada_b.reshape(depth, 1, n))


def _halo_specs(tm, l, d):
    per = tm // HALO
    last = l // HALO - 1
    return [pl.BlockSpec((1, tm, d), lambda b, i: (b, i, 0)),
            pl.BlockSpec((1, HALO, d), lambda b, i: (b, jnp.maximum(i * per - 1, 0), 0)),
            pl.BlockSpec((1, HALO, d), lambda b, i: (b, jnp.minimum((i + 1) * per, last), 0))]


def _halo_rows(x_ref, xp_ref, xn_ref, nw, sc, sh):
    i = pl.program_id(1)
    has_prev = (i > 0).astype(F32)
    has_next = (i < pl.num_programs(1) - 1).astype(F32)
    hp = _norm_mod(xp_ref[0], nw, sc, sh) * has_prev
    hm = _norm_mod(x_ref[0], nw, sc, sh)
    hn = _norm_mod(xn_ref[0], nw, sc, sh) * has_next
    return jnp.concatenate([hp, hm, hn], axis=0).astype(BF16)


def _conv3(u_ref, tm, w, b):
    return (u_ref[pl.ds(HALO - 1, tm), :] * w[0:1] + u_ref[pl.ds(HALO, tm), :] * w[1:2]
            + u_ref[pl.ds(HALO + 1, tm), :] * w[2:3] + b)


def _conv3_rows(u, tm, w, b):
    rows = u.shape[0]
    y = pltpu.roll(u, 1, 0) * w[0:1] + u * w[1:2] + pltpu.roll(u, rows - 1, 0) * w[2:3] + b
    return y[HALO:HALO + tm]


def _ffn_kernel(x_ref, xp_ref, xn_ref, sc_ref, sh_ref, g_ref, nw_ref, wu_ref, cw_ref, cb_ref, wd_ref, o_ref,
                h_buf, a_buf, *, tm):
    f, fc = FFN_HIDDEN, FFN_COL_CHUNK
    h_buf[...] = _halo_rows(x_ref, xp_ref, xn_ref, nw_ref[...], sc_ref[0], sh_ref[0])
    for c in range(f // fc):
        gs = slice(c * fc, (c + 1) * fc)
        vs = slice(f + c * fc, f + (c + 1) * fc)
        gate = _conv3_rows(_dot(h_buf[...], wu_ref[:, gs]), tm, cw_ref[:, gs], cb_ref[:, gs])
        val = _conv3_rows(_dot(h_buf[...], wu_ref[:, vs]), tm, cw_ref[:, vs], cb_ref[:, vs])
        a_buf[:, gs] = (_silu(gate) * val).astype(BF16)
    for n in range(D_MODEL // fc):
        cs = slice(n * fc, (n + 1) * fc)
        o_ref[0, :, cs] = x_ref[0, :, cs] + g_ref[0, :, cs] * _dot(a_buf[...], wd_ref[:, cs])


def _conv_ffn_residual(x, sc, sh, g, nw, wts, tm):
    b, l, d = x.shape
    mod = pl.BlockSpec((1, 1, d), _mod_index(sc.shape[0]))
    once = lambda a: pl.BlockSpec(a.shape, lambda b_, i: (0, 0), pipeline_mode=pl.Buffered(1))
    return pl.pallas_call(
        functools.partial(_ffn_kernel, tm=tm),
        grid=(b, l // tm),
        in_specs=_halo_specs(tm, l, d) + [mod, mod, mod, pl.BlockSpec((1, d), lambda b_, i: (0, 0))]
        + [once(w) for w in wts],
        out_specs=pl.BlockSpec((1, tm, d), lambda b_, i: (b_, i, 0)),
        out_shape=jax.ShapeDtypeStruct((b, l, d), F32),
        scratch_shapes=[pltpu.VMEM((tm + 2 * HALO, d), BF16), pltpu.VMEM((tm, FFN_HIDDEN), BF16)],
        compiler_params=_params(2),
        name="conv_ffn",
    )(x, x, x, sc, sh, g, nw, *wts)


def _ffn_weights(w_up, conv_w, conv_b, w_down):
    return w_up.astype(BF16), conv_w, conv_b[None], w_down.astype(BF16)


def _out_proj_kernel(a_ref, x_ref, g_ref, w_ref, o_ref):
    o_ref[0] = x_ref[0] + g_ref[0] * _dot(a_ref[0].astype(BF16), w_ref[...])


def _out_proj_residual(a, x, g, w, tm):
    b, l, d = x.shape
    k = a.shape[-1]
    return pl.pallas_call(
        _out_proj_kernel,
        grid=(b, l // tm),
        in_specs=[pl.BlockSpec((1, tm, k), lambda b_, i: (b_, i, 0)),
                  pl.BlockSpec((1, tm, d), lambda b_, i: (b_, i, 0)),
                  pl.BlockSpec((1, 1, d), _mod_index(g.shape[0])),
                  pl.BlockSpec((k, d), lambda b_, i: (0, 0))],
        out_specs=pl.BlockSpec((1, tm, d), lambda b_, i: (b_, i, 0)),
        out_shape=jax.ShapeDtypeStruct((b, l, d), F32),
        compiler_params=_params(2),
        name="out_proj_residual",
    )(a, x, g, w)


GLA_W1_COLS = GLA_IN + LANES


def _gla_proj_kernel(x_ref, sc_ref, sh_ref, nw_ref, w1_ref, w2_ref, bg_ref, qk_ref, vr_ref, lg_ref):
    h = _norm_mod(x_ref[0], nw_ref[...], sc_ref[0], sh_ref[0]).astype(BF16)
    p = _dot(h, w1_ref[...])
    qk_ref[0] = p[:, :2 * GLA_DK]
    vr_ref[0] = p[:, 2 * GLA_DK:GLA_IN].astype(BF16)
    z = _dot(p[:, GLA_IN:].astype(BF16), w2_ref[...]) + bg_ref[...]
    lg_ref[0] = jax.nn.log_sigmoid(z) * (1.0 / GLA_TAU)


def _gla_project(x, sc, sh, nw, w1, w2, bg, tm):
    b, l, d = x.shape
    mod = pl.BlockSpec((1, 1, d), _mod_index(sc.shape[0]))
    const = lambda a: pl.BlockSpec(a.shape, lambda b_, i: (0, 0))
    return pl.pallas_call(
        _gla_proj_kernel,
        grid=(b, l // tm),
        in_specs=[pl.BlockSpec((1, tm, d), lambda b_, i: (b_, i, 0)), mod, mod, const(nw), const(w1), const(w2),
                  const(bg)],
        out_specs=[pl.BlockSpec((1, tm, 2 * GLA_DK), lambda b_, i: (b_, i, 0)),
                   pl.BlockSpec((1, tm, 2 * GLA_DV), lambda b_, i: (b_, i, 0)),
                   pl.BlockSpec((1, tm, 2 * GLA_DK), lambda b_, i: (b_, i, 0))],
        out_shape=[jax.ShapeDtypeStruct((b, l, 2 * GLA_DK), F32), jax.ShapeDtypeStruct((b, l, 2 * GLA_DV), BF16),
                   jax.ShapeDtypeStruct((b, l, 2 * GLA_DK), F32)],
        compiler_params=_params(2),
        name="gla_in_proj",
    )(x, sc, sh, nw, w1, w2, bg)


def _gla_tables(c):
    n_lev = int(np.log2(c))
    idx = np.arange(c)
    i, j = idx[:, None], idx[None, :]
    mask = np.zeros((2, n_lev + 1, c, c), np.float32)
    sign = np.zeros((2, n_lev, c, LANES), np.float32)
    step = np.zeros((2, n_lev, c, LANES), np.float32)
    for d in range(2):
        p = idx if d == 0 else c - 1 - idx
        pi, pj = p[:, None], p[None, :]
        mask[d, 0] = np.eye(c)
        for lev in range(n_lev):
            m = c >> (lev + 1)
            upper = (p % (2 * m)) >= m
            sign[d, lev] = np.where(upper, 1.0, -1.0)[:, None]
            mask[d, lev + 1] = (pi // (2 * m) == pj // (2 * m)) & ((pi % (2 * m)) >= m) & ((pj % (2 * m)) < m)
            step[d, lev] = (p >= (1 << lev)).astype(np.float32)[:, None]
    return jnp.asarray(sign), jnp.asarray(step), jnp.asarray(mask)


def _gla_exponents(lg, g_buf, sign_ref, step_ref, d, c, n_lev):
    back = d == 1
    g = lg
    for s in range(n_lev):
        sh = 1 << s
        if sh < 8:
            g = g + step_ref[d, s] * pltpu.roll(g, (c - sh) if back else sh, 0)
        elif back:
            g = jnp.concatenate([g[:c - sh] + g[sh:], g[c - sh:]], axis=0)
        else:
            g = jnp.concatenate([g[:sh], g[sh:] + g[:c - sh]], axis=0)
    g_buf[...] = g
    total = g_buf[0:1, :] if back else g_buf[c - 1:c, :]
    row = lax.broadcasted_iota(jnp.int32, (c, LANES), 0)
    levels = []
    for lev in range(n_lev):
        m = c >> (lev + 1)
        if m == 1:
            levels.append(jnp.maximum(sign_ref[d, lev], 0.0) * lg)
            continue
        if 2 * m >= 8:
            ref_rows = [jnp.broadcast_to(g_buf[ob * 2 * m + (m if back else m - 1):ob * 2 * m + (m if back else m - 1) + 1, :],
                                         (2 * m, LANES)) for ob in range(c // (2 * m))]
            big = jnp.concatenate(ref_rows, axis=0) if len(ref_rows) > 1 else ref_rows[0]
        else:
            tgt = m if back else m - 1
            pos = row % (2 * m)
            big = g
            for k in range(2 * m):
                if k != tgt:
                    big = jnp.where(pos == k, pltpu.roll(g, (k - tgt) % c, 0), big)
        levels.append(sign_ref[d, lev] * (g - big))
    return g, total - g, total, levels


def _gla_scan_kernel(qkf_ref, qkb_ref, vf_ref, vb_ref, lgf_ref, lgb_ref, sign_ref, step_ref, mask_ref, s0_ref,
                     of_ref, ob_ref, sT_ref, state, g_buf, *, c, n_lev):
    @pl.when(pl.program_id(1) == 0)
    def _():
        state[...] = s0_ref[:, 0]

    dirs = ((qkf_ref, vf_ref, lgf_ref, of_ref), (qkb_ref, vb_ref, lgb_ref, ob_ref))
    for d, (qk_ref, v_ref, lg_ref, o_ref) in enumerate(dirs):
        for h in range(GLA_HEADS):
            ks = slice(h * GLA_HK, (h + 1) * GLA_HK)
            kks = slice(GLA_DK + h * GLA_HK, GLA_DK + (h + 1) * GLA_HK)
            vs = slice(h * GLA_HV, (h + 1) * GLA_HV)
            q = qk_ref[0, :, ks] * (GLA_HK ** -0.5)
            k = qk_ref[0, :, kks]
            vb = v_ref[0, :, vs]
            g_in, g_out, g_tot, g_lev = _gla_exponents(lg_ref[0, :, ks], g_buf.at[d, h], sign_ref, step_ref, d, c, n_lev)
            att = mask_ref[d, 0] * _dot_nt(q.astype(BF16), k.astype(BF16))
            for lev in range(n_lev):
                e = jnp.exp(g_lev[lev])
                att = att + mask_ref[d, lev + 1] * _dot_nt((q * e).astype(BF16), (k * e).astype(BF16))
            st = state[d, h]
            o = _dot(att.astype(BF16), vb) + _dot_nt((q * jnp.exp(g_in)).astype(BF16), st.astype(BF16))
            o_ref[0, :, vs] = o.astype(o_ref.dtype)
            st = st * jnp.exp(g_tot) + _dot_tn(vb, (k * jnp.exp(g_out)).astype(BF16))
            state[d, h] = st
            sT_ref[d, 0, h] = st


def _gla_scan(qk, vr, lg, tables, s0):
    b, l, _ = qk.shape
    c = SCAN_CHUNK
    n = l // c
    sign, step, mask = tables
    n_lev = mask.shape[1] - 1
    kern = functools.partial(_gla_scan_kernel, c=c, n_lev=n_lev)
    whole = lambda a: pl.BlockSpec(a.shape, lambda b_, i: (0,) * a.ndim)
    fwd = lambda blk: (lambda b_, i: (b_, i, blk))
    bwd = lambda blk: (lambda b_, i: (b_, n - 1 - i, blk))
    st_spec = pl.BlockSpec((2, 1, GLA_HEADS, GLA_HV, GLA_HK), lambda b_, i: (0, b_, 0, 0, 0))
    return pl.pallas_call(
        kern,
        grid=(b, n),
        in_specs=[pl.BlockSpec((1, c, 2 * GLA_DK), fwd(0)), pl.BlockSpec((1, c, 2 * GLA_DK), bwd(0)),
                  pl.BlockSpec((1, c, GLA_DV), fwd(0)), pl.BlockSpec((1, c, GLA_DV), bwd(0)),
                  pl.BlockSpec((1, c, GLA_DK), fwd(0)), pl.BlockSpec((1, c, GLA_DK), bwd(1)),
                  whole(sign), whole(step), whole(mask), st_spec],
        out_specs=[pl.BlockSpec((1, c, GLA_DV), fwd(0)), pl.BlockSpec((1, c, GLA_DV), bwd(0)), st_spec],
        out_shape=[jax.ShapeDtypeStruct((b, l, GLA_DV), BF16), jax.ShapeDtypeStruct((b, l, GLA_DV), BF16),
                   jax.ShapeDtypeStruct((2, b, GLA_HEADS, GLA_HV, GLA_HK), F32)],
        scratch_shapes=[pltpu.VMEM((2, GLA_HEADS, GLA_HV, GLA_HK), F32),
                        pltpu.VMEM((2, GLA_HEADS, c, GLA_HK), F32)],
        compiler_params=_params(2),
        name="gla_scan",
    )(qk, qk, vr, vr, lg, lg, sign, step, mask, s0)


def _gla_finish_kernel(of_ref, ob_ref, r_ref, x_ref, g_ref, nw_ref, w_ref, o_ref):
    o = of_ref[0].astype(F32) + ob_ref[0].astype(F32)
    heads = []
    for h in range(GLA_HEADS):
        oh = o[:, h * GLA_HV:(h + 1) * GLA_HV]
        ms = jnp.mean(oh * oh, axis=-1, keepdims=True)
        heads.append(oh * lax.rsqrt(ms + EPS) * nw_ref[...])
    y = jnp.concatenate(heads, axis=-1) * _silu(r_ref[0].astype(F32))
    o_ref[0] = x_ref[0] + g_ref[0] * _dot(y.astype(BF16), w_ref[...])


def _gla_finish(o_f, o_b, vr, x, g, nw, w, tm):
    b, l, d = x.shape
    return pl.pallas_call(
        _gla_finish_kernel,
        grid=(b, l // tm),
        in_specs=[pl.BlockSpec((1, tm, GLA_DV), lambda b_, i: (b_, i, 0)),
                  pl.BlockSpec((1, tm, GLA_DV), lambda b_, i: (b_, i, 0)),
                  pl.BlockSpec((1, tm, GLA_DV), lambda b_, i: (b_, i, 1)),
                  pl.BlockSpec((1, tm, d), lambda b_, i: (b_, i, 0)),
                  pl.BlockSpec((1, 1, d), _mod_index(g.shape[0])),
                  pl.BlockSpec((1, GLA_HV), lambda b_, i: (0, 0)),
                  pl.BlockSpec((GLA_DV, d), lambda b_, i: (0, 0))],
        out_specs=pl.BlockSpec((1, tm, d), lambda b_, i: (b_, i, 0)),
        out_shape=jax.ShapeDtypeStruct((b, l, d), F32),
        compiler_params=_params(2),
        name="gla_finish",
    )(o_f, o_b, vr, x, g, nw, w)


def _gla_weights(w_in, w_g1, w_g2, b_g, norm_w, w_out):
    d = w_in.shape[0]
    r = GLA_GATE_RANK
    w1 = jnp.concatenate([w_in, w_g1[0], w_g1[1], jnp.zeros((d, LANES - 2 * r), F32)], axis=1).astype(BF16)
    w2 = jnp.zeros((LANES, 2 * GLA_DK), F32)
    w2 = w2.at[0:r, :GLA_DK].set(w_g2[0]).at[r:2 * r, GLA_DK:].set(w_g2[1]).astype(BF16)
    return w1, w2, b_g.reshape(1, 2 * GLA_DK), norm_w.reshape(1, GLA_HV), w_out.astype(BF16)


def _gla_mixer(xc, xl, mod_c, mod_l, nw, wts, tables, ctx_out, tm_l, tm_c):
    w1, w2, bg, norm_w, w_out = wts
    (csc, csh, cg), (sc, sh, g) = mod_c, mod_l
    qkc, vrc, lgc = _gla_project(xc, csc, csh, nw, w1, w2, bg, tm_c)
    qkl, vrl, lgl = _gla_project(xl, sc, sh, nw, w1, w2, bg, tm_l)
    zero = jnp.zeros((2, xc.shape[0], GLA_HEADS, GLA_HV, GLA_HK), F32)
    ocf, ocb, s_ctx = _gla_scan(qkc, vrc, lgc, tables, zero)
    olf, olb, _ = _gla_scan(qkl, vrl, lgl, tables, s_ctx)
    xl = _gla_finish(olf, olb, vrl, xl, g, norm_w, w_out, tm_l)
    if ctx_out:
        xc = _gla_finish(ocf, ocb, vrc, xc, cg, norm_w, w_out, tm_c)
    return xc, xl


MLA_HEAD_PAD = 2 * LANES
MLA_V_PAD = 2 * LANES


def _rot_cols(w):
    half = MLA_ROPE // 2
    return jnp.concatenate([-w[..., half:], w[..., :half]], axis=-1)


def _swap_halves(w):
    half = MLA_ROPE // 2
    return jnp.concatenate([w[..., half:], w[..., :half]], axis=-1)


def _mla_proj_kernel(x_ref, sc_ref, sh_ref, nw_ref, win_ref, qan_ref, kvan_ref, wuq_ref, wukv_ref,
                     qna_ref, qnb_ref, kna_ref, knb_ref, cs_ref, q_ref, k_ref, v_ref):
    h = _norm_mod(x_ref[0], nw_ref[...], sc_ref[0], sh_ref[0]).astype(BF16)
    p = _dot(h, win_ref[...])
    cq = p[:, :MLA_Q_RANK]
    ckv = p[:, MLA_Q_RANK:MLA_Q_RANK + MLA_KV_RANK]
    kpe = p[:, MLA_Q_RANK + MLA_KV_RANK:]
    cq = cq * lax.rsqrt(jnp.mean(cq * cq, axis=-1, keepdims=True) + EPS) * qan_ref[...]
    ckv = ckv * lax.rsqrt(jnp.mean(ckv * ckv, axis=-1, keepdims=True) + EPS) * kvan_ref[...]
    qall = _dot(cq.astype(BF16), wuq_ref[...])
    kvall = _dot(ckv.astype(BF16), wukv_ref[...])
    cs = cs_ref[...]
    lane = lax.broadcasted_iota(jnp.int32, (1, LANES), 1)
    keep = (lane < MLA_ROPE).astype(F32)
    kpe_ss = 0.5 * jnp.sum(kpe * kpe, axis=-1, keepdims=True)
    scale = MLA_QK ** -0.5 * np.log2(np.e)

    ones_sq = jnp.ones((LANES, LANES), BF16)
    kpe_sq = 0.5 * kpe * kpe

    def head(nope, pe, na, nb, mult):
        ss = _dot((nope * nope + 0.5 * pe * pe).astype(BF16), ones_sq)
        s = lax.rsqrt(ss * (1.0 / MLA_QK) + EPS)
        t = pe * s * nb * cs
        rope = (t + pltpu.roll(t, MLA_ROPE, 1)) * keep
        return jnp.concatenate([nope * s * na * mult, rope * mult], axis=-1).astype(BF16)

    for hd in range(MLA_HEADS):
        qh = qall[:, hd * MLA_HEAD_PAD:(hd + 1) * MLA_HEAD_PAD]
        q_ref[0, hd] = head(qh[:, :LANES], qh[:, LANES:], qna_ref[...], qnb_ref[...], scale)
        kh = kvall[:, hd * 2 * LANES:hd * 2 * LANES + LANES]
        ss = _dot((kh * kh + kpe_sq).astype(BF16), ones_sq)
        s = lax.rsqrt(ss * (1.0 / MLA_QK) + EPS)
        t = kpe * s * knb_ref[...] * cs
        rope = (t + pltpu.roll(t, MLA_ROPE, 1)) * keep
        k_ref[0, hd] = jnp.concatenate([kh * s * kna_ref[...], rope], axis=-1).astype(BF16)
        vh = kvall[:, hd * 2 * LANES + LANES:(hd + 1) * 2 * LANES]
        v_ref[0, hd] = jnp.concatenate([vh, jnp.ones_like(vh)], axis=-1).astype(BF16)


def _mla_project(x, sc, sh, nw, wts, cs, tm):
    b, l, d = x.shape
    win, qan, kvan, wuq, wukv, qna, qnb, kna, knb = wts[:9]
    mod = pl.BlockSpec((1, 1, d), _mod_index(sc.shape[0]))
    const = lambda a: pl.BlockSpec(a.shape, lambda b_, i: (0, 0))
    cs_spec = (pl.BlockSpec((tm, LANES), lambda b_, i: (i, 0)) if cs.shape[0] == l
               else pl.BlockSpec((tm, LANES), lambda b_, i: (0, 0)))
    hspec = lambda w: pl.BlockSpec((1, MLA_HEADS, tm, w), lambda b_, i: (b_, 0, i, 0))
    return pl.pallas_call(
        _mla_proj_kernel,
        grid=(b, l // tm),
        in_specs=[pl.BlockSpec((1, tm, d), lambda b_, i: (b_, i, 0)), mod, mod, const(nw), const(win), const(qan),
                  const(kvan), const(wuq), const(wukv), const(qna), const(qnb), const(kna), const(knb), cs_spec],
        out_specs=[hspec(MLA_HEAD_PAD), hspec(MLA_HEAD_PAD), hspec(MLA_V_PAD)],
        out_shape=[jax.ShapeDtypeStruct((b, MLA_HEADS, l, MLA_HEAD_PAD), BF16),
                   jax.ShapeDtypeStruct((b, MLA_HEADS, l, MLA_HEAD_PAD), BF16),
                   jax.ShapeDtypeStruct((b, MLA_HEADS, l, MLA_V_PAD), BF16)],
        compiler_params=_params(2),
        name="mla_project",
    )(x, sc, sh, nw, win, qan, kvan, wuq, wukv, qna, qnb, kna, knb, cs)


def _flash_kernel(*refs, tk, n_kv, has_ctx):
    if has_ctx:
        q_ref, k_ref, v_ref, kc_ref, vc_ref, o_ref = refs
    else:
        q_ref, k_ref, v_ref, o_ref = refs
    q = q_ref[0, 0]

    def update(carry, kj, vj):
        m, acc = carry
        s = _dot_nt(q, kj)
        m_new = jnp.maximum(m, jnp.max(s, axis=-1, keepdims=True))
        p = jnp.exp2(s - m_new)
        return m_new, jnp.exp2(m - m_new) * acc + _dot(p.astype(BF16), vj)

    tq = q.shape[0]
    carry = (jnp.full((tq, 1), -jnp.inf, F32), jnp.zeros((tq, MLA_V_PAD), F32))
    if has_ctx:
        carry = update(carry, kc_ref[0, 0], vc_ref[0, 0])
    for j in range(n_kv):
        carry = update(carry, k_ref[0, 0, j * tk:(j + 1) * tk, :], v_ref[0, 0, j * tk:(j + 1) * tk, :])
    acc = carry[1]
    o_ref[0] = (acc[:, :MLA_V] / acc[:, MLA_V:]).astype(o_ref.dtype)


def _flash_attention(q, k, v, kc, vc, tq, tk):
    b, hds, l, dq = q.shape
    lk = k.shape[2]
    has_ctx = kc is not None
    kern = functools.partial(_flash_kernel, tk=tk, n_kv=lk // tk, has_ctx=has_ctx)
    whole = lambda a: pl.BlockSpec((1, 1) + a.shape[2:], lambda b_, h, i: (b_, h, 0, 0))
    in_specs = [pl.BlockSpec((1, 1, tq, dq), lambda b_, h, i: (b_, h, i, 0)), whole(k), whole(v)]
    args = [q, k, v]
    if has_ctx:
        in_specs += [whole(kc), whole(vc)]
        args += [kc, vc]
    return pl.pallas_call(
        kern,
        grid=(b, hds, l // tq),
        in_specs=in_specs,
        out_specs=pl.BlockSpec((1, tq, MLA_V), lambda b_, h, i: (b_, i, h)),
        out_shape=jax.ShapeDtypeStruct((b, l, hds * MLA_V), BF16),
        compiler_params=_params(3),
        name="mla_flash_attention",
    )(*args)


def _mla_weights(w_in, q_a_norm, w_uq, kv_a_norm, w_ukv, q_norm, k_norm, w_out):
    kpe = w_in[:, MLA_Q_RANK + MLA_KV_RANK:]
    win = jnp.concatenate([w_in, _rot_cols(kpe)], axis=1).astype(BF16)
    wq = w_uq.reshape(MLA_Q_RANK, MLA_HEADS, MLA_QK)
    wq = jnp.concatenate([wq, _rot_cols(wq[..., MLA_NOPE:])], axis=-1)
    wuq = wq.reshape(MLA_Q_RANK, MLA_HEADS * MLA_HEAD_PAD).astype(BF16)
    gains = lambda nrm: (nrm[None, :MLA_NOPE],
                         jnp.concatenate([nrm[MLA_NOPE:], _swap_halves(nrm[MLA_NOPE:])])[None])
    qna, qnb = gains(q_norm)
    kna, knb = gains(k_norm)
    return (win, q_a_norm[None], kv_a_norm[None], wuq, w_ukv.astype(BF16), qna, qnb, kna, knb, w_out.astype(BF16))


def _rope_table(l):
    rows = l // GRID_W
    pos_r = jnp.repeat(jnp.arange(rows, dtype=F32), GRID_W)
    pos_c = jnp.tile(jnp.arange(GRID_W, dtype=F32), rows)
    n_freq = MLA_ROPE // 4
    inv = ROPE_THETA ** (-jnp.arange(n_freq, dtype=F32) / n_freq)
    ang = jnp.concatenate([pos_r[:, None] * inv, pos_c[:, None] * inv], axis=-1)
    return jnp.concatenate([jnp.cos(ang), jnp.cos(ang), jnp.sin(ang), jnp.sin(ang)], axis=-1)


def _mla_mixer(xc, xl, mod_c, mod_l, nw, wts, ctx_out, tm_l, tm_c):
    (csc, csh, cg), (sc, sh, g) = mod_c, mod_l
    w_out = wts[9]
    lc, l = xc.shape[1], xl.shape[1]
    no_rope = jnp.concatenate([jnp.ones((tm_c, LANES // 2), F32), jnp.zeros((tm_c, LANES // 2), F32)], axis=-1)
    qc, kc, vc = _mla_project(xc, csc, csh, nw, wts, no_rope, tm_c)
    ql, kl, vl = _mla_project(xl, sc, sh, nw, wts, _rope_table(l), tm_l)
    tq = min(1024, l)
    al = _flash_attention(ql, kl, vl, kc, vc, tq, min(512, l))
    xl = _out_proj_residual(al, xl, g, w_out, tm_l)
    if ctx_out:
        ac = _flash_attention(qc, kc, vc, None, None, lc, lc)
        xc = _out_proj_residual(ac, xc, cg, w_out, tm_c)
    return xc, xl


def _ssd_proj_kernel(x_ref, xp_ref, xn_ref, sc_ref, sh_ref, nw_ref, wz_ref, wx_ref, wdt_ref, cw_ref, cb_ref,
                     dtb_ref, z_ref, xbc_ref, dt_ref, h_buf, *, tm):
    cc = SSD_COL_CHUNK
    h_buf[...] = _halo_rows(x_ref, xp_ref, xn_ref, nw_ref[...], sc_ref[0], sh_ref[0])
    hm = h_buf[pl.ds(HALO, tm), :]
    z_ref[0] = _dot(hm, wz_ref[...]).astype(z_ref.dtype)
    dt = jax.nn.softplus(_dot(hm, wdt_ref[...]) + dtb_ref[...])
    dt_ref[0, 0] = dt[:, :LANES]
    dt_ref[1, 0] = dt[:, LANES:]
    for c in range(SSM_XBC // cc):
        cs = slice(c * cc, (c + 1) * cc)
        u = _conv3_rows(_dot(h_buf[...], wx_ref[c]), tm, cw_ref[:, cs], cb_ref[:, cs])
        xbc_ref[0, :, cs] = _silu(u).astype(xbc_ref.dtype)


def _ssd_project(x, sc, sh, nw, wts, tm):
    b, l, d = x.shape
    wz, wx, wdt, cw, cb, dtb = wts[:6]
    mod = pl.BlockSpec((1, 1, d), _mod_index(sc.shape[0]))
    const2 = lambda a: pl.BlockSpec(a.shape, lambda b_, i: (0,) * a.ndim)
    kern = functools.partial(_ssd_proj_kernel, tm=tm)
    return pl.pallas_call(
        kern,
        grid=(b, l // tm),
        in_specs=_halo_specs(tm, l, d) + [mod, mod, const2(nw), const2(wz), const2(wx), const2(wdt), const2(cw),
                                           const2(cb), const2(dtb)],
        out_specs=[pl.BlockSpec((1, tm, SSM_INNER), lambda b_, i: (b_, i, 0)),
                   pl.BlockSpec((1, tm, SSM_XBC), lambda b_, i: (b_, i, 0)),
                   pl.BlockSpec((2, 1, tm, LANES), lambda b_, i: (0, b_, i, 0))],
        out_shape=[jax.ShapeDtypeStruct((b, l, SSM_INNER), BF16),
                   jax.ShapeDtypeStruct((b, l, SSM_XBC), BF16),
                   jax.ShapeDtypeStruct((2, b, l, LANES), F32)],
        scratch_shapes=[pltpu.VMEM((tm + 2 * HALO, d), BF16)],
        compiler_params=_params(2),
        name="ssd_in_proj_conv",
    )(x, x, x, sc, sh, nw, wz, wx, wdt, cw, cb, dtb)


def _ssd_tables(c):
    idx = np.arange(c)
    i, t = idx[:, None], idx[None, :]
    ones = np.ones((8, c), np.float32)
    fwd = np.concatenate([(t <= i).astype(np.float32), ones], 0)
    bwd = np.concatenate([(t >= i).astype(np.float32), ones], 0)
    mask = np.stack([(t <= i), (t >= i)]).astype(np.float32)
    h = np.arange(SSM_HEADS)
    e_seg = np.zeros((LANES, SSM_HEADS, c), np.float32)
    e_seg[h, h, :] = 1.0
    e_seg[SSM_HEADS + h, h, :] = 1.0
    e_x = np.zeros((LANES, 2, SSM_HEADS, SSM_HEADDIM), np.float32)
    for q in range(2):
        e_x[2 * q * SSM_HEADS + h, q, h, :] = 1.0
        e_x[(2 * q + 1) * SSM_HEADS + h, q, h, :] = 1.0
    return (jnp.asarray(np.stack([fwd, bwd]), BF16), jnp.asarray(mask, F32),
            jnp.asarray(e_seg.reshape(LANES, SSM_HEADS * c), BF16),
            jnp.asarray(e_x.reshape(LANES, 2 * SSM_INNER), BF16))


def _pack_hi_lo(parts):
    out = None
    for n, v in enumerate(parts):
        hi = v.astype(BF16).astype(F32)
        for j, piece in enumerate((hi, v - hi)):
            shift = (2 * n + j) * SSM_HEADS
            piece = piece if shift == 0 else pltpu.roll(piece, shift, 1)
            out = piece if out is None else out + piece
    return out.astype(BF16)


def _ssd_scan_kernel(xf_ref, bf_ref, cf_ref, dtf_ref, xb_ref, bb_ref, cb_ref, dtb_ref, a_ref, tri_ref, mask_ref,
                     eseg_ref, ex_ref, s0_ref, yf_ref, yb_ref, sf_ref, state, *, c):
    @pl.when(pl.program_id(1) == 0)
    def _():
        state[...] = s0_ref[:, 0]

    lane = lax.broadcasted_iota(jnp.int32, (1, LANES), 1)
    valid = (lane < SSM_HEADS).astype(F32)
    left = lane < SSM_HEADDIM
    per_group = SSM_HEADS // SSM_GROUPS
    gw = per_group * SSM_HEADDIM
    dirs = ((xf_ref, bf_ref, cf_ref, dtf_ref, yf_ref), (xb_ref, bb_ref, cb_ref, dtb_ref, yb_ref))
    for d, (x_ref, b_ref, c_ref, dt_ref, y_ref) in enumerate(dirs):
        dt = dt_ref[0, 0] * valid
        cums = _split_dot(tri_ref[d], dt * a_ref[d])
        cum = cums[:c]
        total = cums[c:c + 1]
        e_in = jnp.exp(cum) * valid
        w_out = jnp.exp(total - cum) * dt
        e_tot = jnp.exp(total)
        cum_pk = _pack_hi_lo([cum])
        ew_pk = _pack_hi_lo([e_in, w_out])
        cum_t = jnp.transpose(cum)
        dt_t = jnp.transpose(dt)
        causal = mask_ref[d] > 0.5
        for g in range(SSM_GROUPS):
            bm = b_ref[0, :, g * SSM_STATE:(g + 1) * SSM_STATE]
            cm = c_ref[0, :, g * SSM_STATE:(g + 1) * SSM_STATE]
            scores = _dot_nt(cm, bm)
            sg = state[d, g * gw:(g + 1) * gw, :]
            y_in = _dot_nt(cm, sg.astype(BF16))
            x_g = x_ref[0, :, g * gw:(g + 1) * gw]
            cum_col = _dot(cum_pk, eseg_ref[:, g * per_group * c:(g + 1) * per_group * c])
            e_in_x = _dot(ew_pk, ex_ref[:, g * gw:(g + 1) * gw])
            w_out_x = _dot(ew_pk, ex_ref[:, SSM_INNER + g * gw:SSM_INNER + (g + 1) * gw])
            for pr in range(per_group // 2):
                h0 = g * per_group + 2 * pr
                col = g * gw + pr * LANES
                ms = []
                for j in range(2):
                    h = h0 + j
                    seg = jnp.exp(cum_col[:, (2 * pr + j) * c:(2 * pr + j + 1) * c] - cum_t[h:h + 1, :])
                    ms.append((scores * jnp.where(causal, seg, 0.0) * dt_t[h:h + 1, :]).astype(BF16))
                xp = x_g[:, pr * LANES:(pr + 1) * LANES]
                zero = jnp.zeros_like(xp)
                xbd = jnp.concatenate([jnp.where(left, xp, zero), jnp.where(left, zero, xp)], axis=0)
                y = _dot(jnp.concatenate(ms, axis=1), xbd)
                y = y + e_in_x[:, pr * LANES:(pr + 1) * LANES] * y_in[:, pr * LANES:(pr + 1) * LANES]
                y_ref[0, :, col:col + LANES] = y.astype(y_ref.dtype)
            wx = (x_g.astype(F32) * w_out_x).astype(BF16)
            upd = _dot_tn(wx, bm)
            dec = jnp.concatenate(
                [jnp.broadcast_to(e_tot[:, g * per_group + j:g * per_group + j + 1], (SSM_HEADDIM, SSM_STATE))
                 for j in range(per_group)], axis=0)
            state[d, g * gw:(g + 1) * gw, :] = sg * dec + upd
    sf_ref[:, 0] = state[...]


def _ssd_scan(xbc, dt2, a_rows, tables, s0):
    b, l, _ = xbc.shape
    c = SCAN_CHUNK
    n = l // c
    tri, mask, e_seg, e_x = tables
    xblk = SSM_INNER // SSM_BC
    kern = functools.partial(_ssd_scan_kernel, c=c)
    fwd = lambda blk: (lambda b_, i: (b_, i, blk))
    bwd = lambda blk: (lambda b_, i: (b_, n - 1 - i, blk))
    stream = lambda at: [pl.BlockSpec((1, c, SSM_INNER), at(0)), pl.BlockSpec((1, c, SSM_BC), at(xblk)),
                         pl.BlockSpec((1, c, SSM_BC), at(xblk + 1))]
    whole = lambda a: pl.BlockSpec(a.shape, lambda b_, i: (0,) * a.ndim)
    st_spec = pl.BlockSpec((2, 1, SSM_INNER, SSM_STATE), lambda b_, i: (0, b_, 0, 0))
    return pl.pallas_call(
        kern,
        grid=(b, n),
        in_specs=(stream(fwd) + [pl.BlockSpec((1, 1, c, LANES), lambda b_, i: (0, b_, i, 0))]
                  + stream(bwd) + [pl.BlockSpec((1, 1, c, LANES), lambda b_, i: (1, b_, n - 1 - i, 0))]
                  + [whole(a_rows), whole(tri), whole(mask), whole(e_seg), whole(e_x), st_spec]),
        out_specs=[pl.BlockSpec((1, c, SSM_INNER), fwd(0)), pl.BlockSpec((1, c, SSM_INNER), bwd(0)), st_spec],
        out_shape=[jax.ShapeDtypeStruct((b, l, SSM_INNER), BF16), jax.ShapeDtypeStruct((b, l, SSM_INNER), BF16),
                   jax.ShapeDtypeStruct((2, b, SSM_INNER, SSM_STATE), F32)],
        scratch_shapes=[pltpu.VMEM((2, SSM_INNER, SSM_STATE), F32)],
        compiler_params=_params(2),
        name="ssd_scan",
    )(xbc, xbc, xbc, dt2, xbc, xbc, xbc, dt2, a_rows, tri, mask, e_seg, e_x, s0)


def _ssd_finish_kernel(yf_ref, yb_ref, xs_ref, z_ref, x_ref, g_ref, dsk_ref, nw_ref, w_ref, o_ref):
    y = yf_ref[0].astype(F32) + yb_ref[0].astype(F32) + xs_ref[0].astype(F32) * dsk_ref[...]
    y = y * _silu(z_ref[0].astype(F32))
    gw = SSM_INNER // SSM_GROUPS
    parts = []
    for gi in range(SSM_GROUPS):
        yg = y[:, gi * gw:(gi + 1) * gw]
        ms = jnp.mean(yg * yg, axis=-1, keepdims=True)
        parts.append(yg * lax.rsqrt(ms + EPS) * nw_ref[:, gi * gw:(gi + 1) * gw])
    yn = jnp.concatenate(parts, axis=-1)
    o_ref[0] = x_ref[0] + g_ref[0] * _dot(yn.astype(BF16), w_ref[...])


def _ssd_finish(y_f, y_b, xbc, z, x, g, dsk, nw, w, tm):
    b, l, d = x.shape
    return pl.pallas_call(
        _ssd_finish_kernel,
        grid=(b, l // tm),
        in_specs=[pl.BlockSpec((1, tm, SSM_INNER), lambda b_, i: (b_, i, 0)),
                  pl.BlockSpec((1, tm, SSM_INNER), lambda b_, i: (b_, i, 0)),
                  pl.BlockSpec((1, tm, SSM_INNER), lambda b_, i: (b_, i, 0)),
                  pl.BlockSpec((1, tm, SSM_INNER), lambda b_, i: (b_, i, 0)),
                  pl.BlockSpec((1, tm, d), lambda b_, i: (b_, i, 0)),
                  pl.BlockSpec((1, 1, d), _mod_index(g.shape[0])),
                  pl.BlockSpec((1, SSM_INNER), lambda b_, i: (0, 0)),
                  pl.BlockSpec((1, SSM_INNER), lambda b_, i: (0, 0)),
                  pl.BlockSpec((SSM_INNER, d), lambda b_, i: (0, 0))],
        out_specs=pl.BlockSpec((1, tm, d), lambda b_, i: (b_, i, 0)),
        out_shape=jax.ShapeDtypeStruct((b, l, d), F32),
        compiler_params=_params(2),
        name="ssd_finish",
    )(y_f, y_b, xbc, z, x, g, dsk, nw, w)


def _ssd_weights(w_in, conv_w, conv_b, dt_bias, a_log, d_skip, norm_w, w_out):
    d = w_in.shape[0]
    wz = w_in[:, :SSM_INNER].astype(BF16)
    wx = w_in[:, SSM_INNER:SSM_INNER + SSM_XBC].reshape(d, SSM_XBC // SSD_COL_CHUNK, SSD_COL_CHUNK)
    wx = jnp.transpose(wx, (1, 0, 2)).astype(BF16)
    wdt_raw = w_in[:, SSM_INNER + SSM_XBC:]
    pad = jnp.zeros((d, LANES - SSM_HEADS), F32)
    wdt = jnp.concatenate([wdt_raw[:, :SSM_HEADS], pad, wdt_raw[:, SSM_HEADS:], pad], axis=1).astype(BF16)
    bpad = jnp.zeros((LANES - SSM_HEADS,), F32)
    dtb = jnp.concatenate([dt_bias[0], bpad, dt_bias[1], bpad])[None]
    cw, cb = conv_w, conv_b[None]
    a = -jnp.exp(a_log.astype(F32))
    a_rows = jnp.concatenate([a, jnp.zeros((2, LANES - SSM_HEADS), F32)], axis=1).reshape(2, 1, LANES)
    dsk = jnp.repeat(d_skip, SSM_HEADDIM)[None]
    return wz, wx, wdt, cw, cb, dtb, a_rows, dsk, norm_w[None], w_out.astype(BF16)


def _ssd_mixer(xc, xl, mod_c, mod_l, nw, wts, tables, ctx_out, tm_l, tm_c):
    a_rows, dsk, norm_w, w_out = wts[6:]
    (csc, csh, cg), (sc, sh, g) = mod_c, mod_l
    zc, xbc_c, dtc = _ssd_project(xc, csc, csh, nw, wts, tm_c)
    zl, xbc_l, dtl = _ssd_project(xl, sc, sh, nw, wts, tm_l)
    zero = jnp.zeros((2, xc.shape[0], SSM_INNER, SSM_STATE), F32)
    ycf, ycb, s_ctx = _ssd_scan(xbc_c, dtc, a_rows, tables, zero)
    ylf, ylb, _ = _ssd_scan(xbc_l, dtl, a_rows, tables, s_ctx)
    xl = _ssd_finish(ylf, ylb, xbc_l, zl, xl, g, dsk, norm_w, w_out, tm_l)
    if ctx_out:
        xc = _ssd_finish(ycf, ycb, xbc_c, zc, xc, cg, dsk, norm_w, w_out, tm_c)
    return xc, xl


def kernel(x, c, ctx, c_ctx, ada_w, ada_b, norm_mix_w, norm_ffn_w, ffn_w_up, ffn_conv_w, ffn_conv_b, ffn_w_down, gla_w_in, gla_w_g1, gla_w_g2, gla_b_g, gla_norm_w, gla_w_out, mla_w_in, mla_q_a_norm, mla_w_uq, mla_kv_a_norm, mla_w_ukv, mla_q_norm, mla_k_norm, mla_w_out, ssm_w_in, ssm_conv_w, ssm_conv_b, ssm_dt_bias, ssm_a_log, ssm_d, ssm_norm_w, ssm_w_out):
    bsz, l, d = x.shape
    lc = ctx.shape[1]
    tm_l = min(512, l)
    tm_c = lc
    cond = jnp.concatenate([c, c_ctx[None], jnp.zeros((8 - bsz - 1, d), F32)], axis=0)
    mods = _modulation(cond, ada_w, ada_b)
    gla_tables = _gla_tables(SCAN_CHUNK)
    ssd_tables = _ssd_tables(SCAN_CHUNK)
    xl, xc = x, ctx
    for i in range(DEPTH):
        kind, j = i % N_MIXERS, i // N_MIXERS
        ctx_out = i < DEPTH - 1
        m6 = [mods[i, :, k * d:(k + 1) * d] for k in range(N_MOD)]
        lat = [m[:bsz, None, :] for m in m6]
        cx = [m[bsz:bsz + 1, None, :] for m in m6]
        nw = norm_mix_w[i][None]
        mod_l, mod_c = (lat[1], lat[0], lat[2]), (cx[1], cx[0], cx[2])
        if kind == 0:
            wts = _gla_weights(gla_w_in[j], gla_w_g1[j], gla_w_g2[j], gla_b_g[j], gla_norm_w[j], gla_w_out[j])
            xc, xl = _gla_mixer(xc, xl, mod_c, mod_l, nw, wts, gla_tables, ctx_out, tm_l, tm_c)
        elif kind == 1:
            wts = _mla_weights(mla_w_in[j], mla_q_a_norm[j], mla_w_uq[j], mla_kv_a_norm[j], mla_w_ukv[j],
                               mla_q_norm[j], mla_k_norm[j], mla_w_out[j])
            xc, xl = _mla_mixer(xc, xl, mod_c, mod_l, nw, wts, ctx_out, tm_l, tm_c)
        else:
            wts = _ssd_weights(ssm_w_in[j], ssm_conv_w[j], ssm_conv_b[j], ssm_dt_bias[j], ssm_a_log[j], ssm_d[j],
                               ssm_norm_w[j], ssm_w_out[j])
            xc, xl = _ssd_mixer(xc, xl, mod_c, mod_l, nw, wts, ssd_tables, ctx_out, tm_l, tm_c)
        fw = _ffn_weights(ffn_w_up[i], ffn_conv_w[i], ffn_conv_b[i], ffn_w_down[i])
        nfw = norm_ffn_w[i][None]
        xl = _conv_ffn_residual(xl, lat[4], lat[3], lat[5], nfw, fw, min(FFN_ROWS, l))
        if ctx_out:
            xc = _conv_ffn_residual(xc, cx[4], cx[3], cx[5], nfw, fw, tm_c)
    return xl
```
